```python
import jax, jax.numpy as jnp
from jax import lax
import numpy as np

D_MODEL = 2048
BATCH = 4
SEQ = 2048
DEPTH = 1
DEC_BATCH = 32
DEC_SEQ = 1
PAST_LEN = 16384
PAGE_SIZE = 128

N_ATTN_HEADS = 8
ATTN_HEAD_DIM = 128
D_ATTN = N_ATTN_HEADS * ATTN_HEAD_DIM
D_SSM = D_MODEL
SSM_HEAD_DIM = 64
N_SSM_HEADS = D_SSM // SSM_HEAD_DIM
N_SSM_GROUPS = 4
SSM_STATE = 128
CONV_WIDTH = 4
D_CONV = D_SSM + 2 * N_SSM_GROUPS * SSM_STATE
D_MIX = D_ATTN + D_SSM
D_IN = 3 * D_ATTN + N_ATTN_HEADS + D_SSM + D_CONV + N_SSM_HEADS
D_FF = ((8 * D_MODEL + 768 - 1) // 768) * 256
Q_BLOCK = 128
SSD_CHUNK = 128
EPS = 1e-6

kernel_name = 'hymba_fox_ssd_decode_step'


def rmsnorm(x, g):
    xf = x.astype(jnp.float32)
    y = xf * lax.rsqrt(jnp.mean(xf * xf, axis=-1, keepdims=True) + EPS)
    return (y * g.astype(jnp.float32)).astype(x.dtype)


def fox_block(q, k_blk, v_blk, lf_blk, prefix):
    t = q.shape[1]
    scale = ATTN_HEAD_DIM ** -0.5
    a = jnp.cumsum(lf_blk, axis=1)
    a_h = jnp.swapaxes(a, 1, 2)
    causal = jnp.tril(jnp.ones((t, t), dtype=bool))
    s_blk = jnp.einsum('bthd,bshd->bhts', q, k_blk, preferred_element_type=jnp.float32) * scale - a_h[:, :, None, :]
    scores = [jnp.where(causal, s_blk, -jnp.inf)]
    values = [v_blk]
    later = jnp.zeros_like(a[:, :1])
    for k_p, v_p, lf_p in reversed(prefix):
        if k_p.shape[1] == 0:
            continue
        rc = lax.cumsum(lf_p, axis=1, reverse=True)
        kb = jnp.concatenate([rc[:, 1:], jnp.zeros_like(rc[:, :1])], axis=1) + later
        later = later + rc[:, :1]
        s_p = jnp.einsum('bthd,bshd->bhts', q, k_p, preferred_element_type=jnp.float32) * scale
        scores.insert(0, s_p + jnp.swapaxes(kb, 1, 2)[:, :, None, :])
        values.insert(0, v_p)
    s = jnp.concatenate(scores, axis=-1) + a_h[..., None]
    p = jax.nn.softmax(s, axis=-1)
    out = 0.0
    off = 0
    for v_p in values:
        n = v_p.shape[1]
        out = out + jnp.einsum('bhts,bshd->bthd', p[..., off:off + n].astype(v_p.dtype), v_p,
                               preferred_element_type=jnp.float32)
        off += n
    return out


def fox_attention(q, k, v, lf, past):
    t = q.shape[1]
    outs = []
    for q0 in range(0, t, Q_BLOCK):
        q1 = min(q0 + Q_BLOCK, t)
        prefix = list(past) + [(k[:, :q0], v[:, :q0], lf[:, :q0])]
        outs.append(fox_block(q[:, q0:q1], k[:, q0:q1], v[:, q0:q1], lf[:, q0:q1], prefix))
    return jnp.concatenate(outs, axis=1)


def ssd_scan(xh, dt, a, bm, cm, h0):
    b, t = xh.shape[:2]
    l = SSD_CHUNK if t >= SSD_CHUNK else t
    pad = (-t) % l
    if pad:
        xh = jnp.pad(xh, ((0, 0), (0, pad), (0, 0), (0, 0)))
        dt = jnp.pad(dt, ((0, 0), (0, pad), (0, 0)))
        bm = jnp.pad(bm, ((0, 0), (0, pad), (0, 0), (0, 0)))
        cm = jnp.pad(cm, ((0, 0), (0, pad), (0, 0), (0, 0)))
    nc = (t + pad) // l
    hg = N_SSM_HEADS // N_SSM_GROUPS
    xdt = (xh * dt[..., None]).reshape(b, nc, l, N_SSM_GROUPS, hg, SSM_HEAD_DIM)
    cum = jnp.cumsum((dt * a).reshape(b, nc, l, N_SSM_GROUPS, hg), axis=2)
    bc = bm.reshape(b, nc, l, N_SSM_GROUPS, SSM_STATE)
    cc = cm.reshape(b, nc, l, N_SSM_GROUPS, SSM_STATE)
    causal = jnp.tril(jnp.ones((l, l), dtype=bool))[:, :, None, None]
    seg = cum[:, :, :, None] - cum[:, :, None]
    decay_ls = jnp.exp(jnp.where(causal, seg, -jnp.inf))
    cb = jnp.einsum('bclgn,bcsgn->bclsg', cc, bc)
    y_diag = jnp.einsum('bclsg,bclsgj,bcsgjp->bclgjp', cb, decay_ls, xdt)
    decay_end = jnp.exp(cum[:, :, -1:] - cum)
    states = jnp.einsum('bclgn,bclgj,bclgjp->bcgjpn', bc, decay_end, xdt)
    chunk_decay = jnp.exp(cum[:, :, -1])

    def step(h, inp):
        s_c, d_c = inp
        return d_c[..., None, None] * h + s_c, h

    h_last, h_in = lax.scan(step, h0.reshape(b, N_SSM_GROUPS, hg, SSM_HEAD_DIM, SSM_STATE),
                            (jnp.moveaxis(states, 1, 0), jnp.moveaxis(chunk_decay, 1, 0)))
    h_in = jnp.moveaxis(h_in, 0, 1)
    y_off = jnp.einsum('bclgn,bcgjpn,bclgj->bclgjp', cc, h_in, jnp.exp(cum))
    y = (y_diag + y_off).reshape(b, nc * l, N_SSM_HEADS, SSM_HEAD_DIM)[:, :t]
    return y, h_last.reshape(b, N_SSM_HEADS, SSM_HEAD_DIM, SSM_STATE)


def mixer(h, w_in, b_fgate, conv_w, conv_b, dt_bias, a_log, d_skip, g_attn, g_ssd, w_out,
          attn_past, conv_prev, ssm_prev):
    b, t, _ = h.shape
    f32 = jnp.float32
    split_pts = [int(s) for s in np.cumsum([D_ATTN, D_ATTN, D_ATTN, N_ATTN_HEADS, D_SSM, D_CONV])]
    q, k, v, f_raw, z, xbc, dt_raw = jnp.split(h @ w_in, split_pts, axis=-1)
    hs = (b, t, N_ATTN_HEADS, ATTN_HEAD_DIM)
    q, k, v = q.reshape(hs), k.reshape(hs), v.reshape(hs)
    lf = jax.nn.log_sigmoid(f_raw.astype(f32) + b_fgate.astype(f32))
    o_attn = fox_attention(q, k, v, lf, attn_past).reshape(b, t, D_ATTN)
    o_attn = rmsnorm(o_attn, g_attn).astype(h.dtype)
    xpad = jnp.concatenate([conv_prev.astype(xbc.dtype), xbc], axis=1)
    new_conv = xpad[:, t:]
    xpf = xpad.astype(f32)
    conv = conv_b.astype(f32) + sum(conv_w[i].astype(f32) * xpf[:, i:i + t] for i in range(CONV_WIDTH))
    xbc_act = jax.nn.silu(conv)
    xs, bm, cm = jnp.split(xbc_act, [D_SSM, D_SSM + N_SSM_GROUPS * SSM_STATE], axis=-1)
    dt = jax.nn.softplus(dt_raw.astype(f32) + dt_bias.astype(f32))
    a = -jnp.exp(a_log.astype(f32))
    xh = xs.reshape(b, t, N_SSM_HEADS, SSM_HEAD_DIM)
    y, ssm_new = ssd_scan(xh, dt, a, bm.reshape(b, t, N_SSM_GROUPS, SSM_STATE),
                          cm.reshape(b, t, N_SSM_GROUPS, SSM_STATE), ssm_prev.astype(f32))
    y = (y + d_skip.astype(f32)[:, None] * xh).reshape(b, t, D_SSM) * jax.nn.silu(z.astype(f32))
    o_ssm = rmsnorm(y, g_ssd).astype(h.dtype)
    out = jnp.concatenate([o_attn, o_ssm], axis=-1) @ w_out
    return out, k, v, lf, new_conv, ssm_new


def trunk_layer(x, g_mix, w_in, b_fgate, conv_w, conv_b, dt_bias, a_log, d_skip, g_attn, g_ssd, w_out,
                g_ffn, w_gate, w_up, w_down, attn_past, conv_prev, ssm_prev):
    m, k, v, lf, conv_new, ssm_new = mixer(rmsnorm(x, g_mix), w_in, b_fgate, conv_w, conv_b, dt_bias, a_log,
                                           d_skip, g_attn, g_ssd, w_out, attn_past, conv_prev, ssm_prev)
    x = x + m
    hf = rmsnorm(x, g_ffn)
    x = x + (jax.nn.silu(hf @ w_gate) * (hf @ w_up)) @ w_down
    return x, k, v, lf, conv_new, ssm_new


def setup_inputs(seed: int = 0) -> dict:
    key = jax.random.key(seed)
    ks = jax.random.split(key, 32)
    f32 = jnp.float32
    n_pages = PAST_LEN // PAGE_SIZE
    n_used = DEC_BATCH * n_pages
    n_pool = n_used + max(1, n_used // 4)

    def nrm(k, shape, scale):
        return jax.random.normal(k, shape, f32) * scale

    dt0 = jnp.exp(jax.random.uniform(ks[12], (DEPTH, N_SSM_HEADS), f32, np.log(1e-3), np.log(1e-1)))
    return {
        'x_prompt': nrm(ks[0], (BATCH, SEQ, D_MODEL), 1.0),
        'x_sample': nrm(ks[1], (DEC_BATCH, DEC_SEQ, D_MODEL), 1.0),
        'cache_k': nrm(ks[2], (DEPTH, n_pool, PAGE_SIZE, N_ATTN_HEADS, ATTN_HEAD_DIM), 1.0),
        'cache_v': nrm(ks[3], (DEPTH, n_pool, PAGE_SIZE, N_ATTN_HEADS, ATTN_HEAD_DIM), 1.0),
        'cache_logf': jax.nn.log_sigmoid(9.0 + nrm(ks[4], (DEPTH, n_pool, PAGE_SIZE, N_ATTN_HEADS), 0.5)),
        'state_ssm': nrm(ks[5], (DEPTH, DEC_BATCH, N_SSM_HEADS, SSM_HEAD_DIM, SSM_STATE), 0.1),
        'state_conv': nrm(ks[6], (DEPTH, DEC_BATCH, CONV_WIDTH - 1, D_CONV), 1.0),
        'page_table': jax.random.permutation(ks[7], n_pool)[:n_used].reshape(DEC_BATCH, n_pages).astype(jnp.int32),
        'g_mix': 1.0 + nrm(ks[8], (DEPTH, D_MODEL), 0.02),
        'w_in': nrm(ks[9], (DEPTH, D_MODEL, D_IN), D_MODEL ** -0.5),
        'b_fgate': jax.random.uniform(ks[10], (DEPTH, N_ATTN_HEADS), f32, 3.0, 8.0),
        'conv_w': nrm(ks[11], (DEPTH, CONV_WIDTH, D_CONV), CONV_WIDTH ** -0.5),
        'conv_b': nrm(ks[13], (DEPTH, D_CONV), 0.01),
        'dt_bias': dt0 + jnp.log(-jnp.expm1(-dt0)),
        'a_log': jnp.log(jax.random.uniform(ks[14], (DEPTH, N_SSM_HEADS), f32, 1.0, 16.0)),
        'd_skip': 1.0 + nrm(ks[15], (DEPTH, N_SSM_HEADS), 0.02),
        'g_attn': 1.0 + nrm(ks[16], (DEPTH, D_ATTN), 0.02),
        'g_ssd': 1.0 + nrm(ks[17], (DEPTH, D_SSM), 0.02),
        'w_out': nrm(ks[18], (DEPTH, D_MIX, D_MODEL), D_MIX ** -0.5),
        'g_ffn': 1.0 + nrm(ks[19], (DEPTH, D_MODEL), 0.02),
        'w_gate': nrm(ks[20], (DEPTH, D_MODEL, D_FF), D_MODEL ** -0.5),
        'w_up': nrm(ks[21], (DEPTH, D_MODEL, D_FF), D_MODEL ** -0.5),
        'w_down': nrm(ks[22], (DEPTH, D_FF, D_MODEL), D_FF ** -0.5),
        'g_final': 1.0 + nrm(ks[23], (D_MODEL,), 0.02),
    }


def reference(x_prompt, x_sample, cache_k, cache_v, cache_logf, state_ssm, state_conv, page_table,
              g_mix, w_in, b_fgate, conv_w, conv_b, dt_bias, a_log, d_skip, g_attn, g_ssd, w_out,
              g_ffn, w_gate, w_up, w_down, g_final):
    bp = x_prompt.shape[0]
    bs = x_sample.shape[0]
    past_len = page_table.shape[1] * PAGE_SIZE
    yp, ys = x_prompt, x_sample
    kp_l, vp_l, lfp_l, cp_l, sp_l = [], [], [], [], []
    ks_l, vs_l, lfs_l, cs_l, ss_l = [], [], [], [], []
    for l in range(DEPTH):
        lw = (g_mix[l], w_in[l], b_fgate[l], conv_w[l], conv_b[l], dt_bias[l], a_log[l], d_skip[l],
              g_attn[l], g_ssd[l], w_out[l], g_ffn[l], w_gate[l], w_up[l], w_down[l])
        conv0 = jnp.zeros((bp, CONV_WIDTH - 1, D_CONV), x_prompt.dtype)
        ssm0 = jnp.zeros((bp, N_SSM_HEADS, SSM_HEAD_DIM, SSM_STATE), jnp.float32)
        yp, kp, vp, lfp, cp, sp = trunk_layer(yp, *lw, [], conv0, ssm0)
        k_past = cache_k[l][page_table].reshape(bs, past_len, N_ATTN_HEADS, ATTN_HEAD_DIM)
        v_past = cache_v[l][page_table].reshape(bs, past_len, N_ATTN_HEADS, ATTN_HEAD_DIM)
        lf_past = cache_logf[l][page_table].reshape(bs, past_len, N_ATTN_HEADS).astype(jnp.float32)
        ys, k_s, v_s, lf_s, c_s, s_s = trunk_layer(ys, *lw, [(k_past, v_past, lf_past)], state_conv[l], state_ssm[l])
        kp_l.append(kp); vp_l.append(vp); lfp_l.append(lfp); cp_l.append(cp); sp_l.append(sp)
        ks_l.append(k_s); vs_l.append(v_s); lfs_l.append(lf_s); cs_l.append(c_s); ss_l.append(s_s)
    y_prompt = rmsnorm(yp, g_final)
    y_sample = rmsnorm(ys, g_final)
    return (y_prompt, y_sample,
            jnp.stack(kp_l), jnp.stack(vp_l), jnp.stack(lfp_l),
            jnp.stack(ks_l), jnp.stack(vs_l), jnp.stack(lfs_l),
            jnp.stack(sp_l), jnp.stack(cp_l),
            jnp.stack(ss_l), jnp.stack(cs_l))
```

```python
import functools

import numpy as np
import jax
import jax.numpy as jnp
from jax import lax
from jax.experimental import pallas as pl
from jax.experimental.pallas import tpu as pltpu

F32 = jnp.float32
BF16 = jnp.bfloat16
EPS = 1e-6
LANES = 128
SUBLANES = 8
SSD_CHUNK = 128
VMEM_LIMIT_MB = 56

_NT = (((1,), (1,)), ((), ()))
_TN = (((0,), (0,)), ((), ()))


def _params(sem):
    return pltpu.CompilerParams(dimension_semantics=sem, vmem_limit_bytes=VMEM_LIMIT_MB * 2**20)


def _resident(shape):
    nd = len(shape)
    return pl.BlockSpec(shape, lambda *_: (0,) * nd, pipeline_mode=pl.Buffered(1))


def _silu(x):
    return x / (1.0 + jnp.exp(-x))


def _split3(x):
    hi = x.astype(BF16)
    r1 = x - hi.astype(F32)
    mid = r1.astype(BF16)
    lo = (r1 - mid.astype(F32)).astype(BF16)
    return hi, mid, lo


def _dot_r01(a, b01):
    hi, mid, lo = _split3(a)
    d = functools.partial(jnp.dot, preferred_element_type=F32)
    return d(hi, b01) + d(mid, b01) + d(lo, b01)


def _dot_l01(a01, b):
    hi, mid, lo = _split3(b)
    d = functools.partial(jnp.dot, preferred_element_type=F32)
    return d(a01, hi) + d(a01, mid) + d(a01, lo)


def _norm_proj_body(x_ref, g_ref, *refs, n_w, out_map, tn):
    w_refs, o_refs = refs[:n_w], refs[n_w:]
    x = x_ref[...]
    h = (x * lax.rsqrt(jnp.mean(x * x, axis=-1, keepdims=True) + EPS) * g_ref[...]).astype(BF16)
    for wi, w_ref in enumerate(w_refs):
        n = w_ref.shape[1]
        for n0 in range(0, n, tn):
            n1 = min(n0 + tn, n)
            r = jnp.dot(h, w_ref[:, n0:n1], preferred_element_type=F32)
            for oi in out_map[wi]:
                o_refs[oi][:, n0:n1] = r.astype(o_refs[oi].dtype)


def _norm_proj(x, g, weights, out_map, out_dtypes, tm, name):
    m, d = x.shape
    widths = {}
    for wi, ois in enumerate(out_map):
        for oi in ois:
            widths[oi] = weights[wi].shape[1]
    n_out = len(out_dtypes)
    body = functools.partial(_norm_proj_body, n_w=len(weights), out_map=out_map, tn=512)
    return pl.pallas_call(
        body,
        grid=(m // tm,),
        in_specs=[pl.BlockSpec((tm, d), lambda i: (i, 0)), _resident((1, d))]
        + [_resident(w.shape) for w in weights],
        out_specs=[pl.BlockSpec((tm, widths[oi]), lambda i: (i, 0)) for oi in range(n_out)],
        out_shape=[jax.ShapeDtypeStruct((m, widths[oi]), out_dtypes[oi]) for oi in range(n_out)],
        compiler_params=_params(("arbitrary",)),
        name=name,
    )(x, g, *weights)


def _gates_body(fd_ref, bias_ref, arow_ref, lf_ref, dt_ref, cs_ref, cst_ref, *, n_heads):
    t = fd_ref.shape[0]
    L = SSD_CHUNK
    lane = lax.broadcasted_iota(jnp.int32, (L, LANES), 1)
    tri = (lax.broadcasted_iota(jnp.int32, (L, L), 0) >= lax.broadcasted_iota(jnp.int32, (L, L), 1)).astype(BF16)
    is_f = lane < n_heads
    carry = jnp.zeros((1, LANES), F32)
    for c in range(t // L):
        rows = slice(c * L, (c + 1) * L)
        v = fd_ref[rows, :] + bias_ref[...]
        sp = jnp.log1p(jnp.exp(-jnp.abs(v)))
        lf = jnp.minimum(v, 0.0) - sp
        dt = jnp.maximum(v, 0.0) + sp
        zc = jnp.where(is_f, lf, dt * arow_ref[...])
        cs = _dot_l01(tri, zc) + jnp.where(is_f, carry, 0.0)
        carry = cs[L - 1:L, :]
        lf_ref[rows, :] = lf
        dt_ref[rows, :] = dt
        cs_ref[rows, :] = cs
        cst_ref[:, rows] = cs.T


def _gates(fd, bias_row, a_row, batch, seq, n_heads):
    m = fd.shape[0]
    row = lambda b: (b, 0)
    return pl.pallas_call(
        functools.partial(_gates_body, n_heads=n_heads),
        grid=(batch,),
        in_specs=[pl.BlockSpec((seq, LANES), row), _resident((1, LANES)), _resident((1, LANES))],
        out_specs=[pl.BlockSpec((seq, LANES), row)] * 3 + [pl.BlockSpec((None, LANES, seq), lambda b: (b, 0, 0))],
        out_shape=[jax.ShapeDtypeStruct((m, LANES), F32)] * 3 + [jax.ShapeDtypeStruct((batch, LANES, seq), F32)],
        compiler_params=_params(("arbitrary",)),
        name="gates",
    )(fd, bias_row, a_row)


def _sgates_body(fd_ref, bias_ref, lf_ref, dt_ref):
    v = fd_ref[...] + bias_ref[...]
    sp = jnp.log1p(jnp.exp(-jnp.abs(v)))
    lf_ref[...] = jnp.minimum(v, 0.0) - sp
    dt_ref[...] = jnp.maximum(v, 0.0) + sp


def _sgates(fd, bias_row):
    return pl.pallas_call(
        _sgates_body,
        out_shape=[jax.ShapeDtypeStruct(fd.shape, F32)] * 2,
        name="sample_gates",
    )(fd, bias_row)


def _attn_body(q_ref, k_ref, v_ref, ct_ref, g_ref, o_ref, acc_ref, *, tq, n_heads, dh, scale):
    qi = pl.program_id(1)
    causal = lax.broadcasted_iota(jnp.int32, (tq, tq), 0) >= lax.broadcasted_iota(jnp.int32, (tq, tq), 1)
    for h in range(n_heads):
        sl = slice(h * dh, (h + 1) * dh)
        q = q_ref[:, sl]

        def kv_step(kj, carry, masked, q=q, sl=sl, h=h):
            m, l, acc = carry
            k0 = pl.multiple_of(kj * tq, tq)
            k = k_ref[pl.ds(k0, tq), sl]
            v = v_ref[pl.ds(k0, tq), sl]
            s = lax.dot_general(q, k, _NT, preferred_element_type=F32) * scale
            s = s - ct_ref[h:h + 1, pl.ds(k0, tq)]
            if masked:
                s = jnp.where(causal, s, -jnp.inf)
            m_new = jnp.maximum(m, jnp.max(s, axis=-1, keepdims=True))
            alpha = jnp.exp(m - m_new)
            p = jnp.exp(s - m_new)
            l = alpha * l + jnp.sum(p, axis=-1, keepdims=True)
            acc = alpha * acc + jnp.dot(p.astype(BF16), v, preferred_element_type=F32)
            return m_new, l, acc

        init = (jnp.full((tq, 1), -jnp.inf, F32), jnp.zeros((tq, 1), F32), jnp.zeros((tq, dh), F32))
        carry = lax.fori_loop(0, qi, lambda kj, c: kv_step(kj, c, False), init)
        m, l, acc = kv_step(qi, carry, True)
        acc_ref[:, sl] = acc / l
    o = acc_ref[...]
    o_ref[...] = (o * lax.rsqrt(jnp.mean(o * o, axis=-1, keepdims=True) + EPS) * g_ref[...]).astype(o_ref.dtype)


def _attention(q, kb, vb, cst, g_attn, batch, seq, n_heads, dh, tq):
    m, da = q.shape
    nq = seq // tq
    body = functools.partial(_attn_body, tq=tq, n_heads=n_heads, dh=dh, scale=dh ** -0.5)
    return pl.pallas_call(
        body,
        grid=(batch, nq),
        in_specs=[
            pl.BlockSpec((tq, da), lambda b, i: (b * nq + i, 0)),
            pl.BlockSpec((seq, da), lambda b, i: (b, 0)),
            pl.BlockSpec((seq, da), lambda b, i: (b, 0)),
            pl.BlockSpec((None, SUBLANES, seq), lambda b, i: (b, 0, 0)),
            _resident((1, da)),
        ],
        out_specs=pl.BlockSpec((tq, da), lambda b, i: (b * nq + i, 0)),
        out_shape=jax.ShapeDtypeStruct((m, da), BF16),
        scratch_shapes=[pltpu.VMEM((tq, da), F32)],
        compiler_params=_params(("arbitrary", "arbitrary")),
        name="fox_attention",
    )(q, kb, vb, cst, g_attn)


def _ssd_body(xbc_ref, z_ref, dt_ref, cs_ref, cst_ref, cw_ref, cb_ref, dsk_ref, g_ref,
              o_ref, st_ref, xp_ref, act_ref, y_ref, *, ds, n_groups, n_state, n_heads, p, hoff, kw):
    L = SSD_CHUNK
    dc = xbc_ref.shape[1]
    c = pl.program_id(1)
    pad = SUBLANES

    @pl.when(c == 0)
    def _():
        st_ref[...] = jnp.zeros_like(st_ref)
        xp_ref[0:pad, :] = jnp.zeros((pad, dc), F32)

    xp_ref[pad:pad + L, :] = xbc_ref[...]
    slab = 512
    for c0 in range(0, dc, slab):
        sl = slice(c0, c0 + slab)
        a = cw_ref[0:1, sl] * xp_ref[pad - kw + 1:pad - kw + 1 + L, sl]
        for i in range(1, kw):
            a = a + cw_ref[i:i + 1, sl] * xp_ref[pad - kw + 1 + i:pad - kw + 1 + i + L, sl]
        a = cb_ref[:, sl] + a
        act_ref[:, sl] = _silu(a)
    xp_ref[0:pad, :] = xp_ref[L:L + pad, :]

    dtv = dt_ref[...]
    cumv = cs_ref[...]
    cumt = cst_ref[...]
    lane = lax.broadcasted_iota(jnp.int32, (L, LANES), 1)
    causal = lax.broadcasted_iota(jnp.int32, (L, L), 0) >= lax.broadcasted_iota(jnp.int32, (L, L), 1)
    first = lane < p
    hpg = n_heads // n_groups
    gw = hpg * p
    for g in range(n_groups):
        bg = act_ref[:, ds + g * n_state:ds + (g + 1) * n_state].astype(BF16)
        cg = act_ref[:, ds + (n_groups + g) * n_state:ds + (n_groups + g + 1) * n_state].astype(BF16)
        cbm = lax.dot_general(cg, bg, _NT, preferred_element_type=F32)
        st_g = st_ref[g * gw:(g + 1) * gw, :]
        yoff = lax.dot_general(cg, st_g.astype(BF16), _NT, preferred_element_type=F32)
        xd_parts = []
        for pr in range(hpg // 2):
            j0 = g * hpg + 2 * pr
            la, lb = hoff + j0, hoff + j0 + 1
            c0v, c1v = cumv[:, la:la + 1], cumv[:, lb:lb + 1]
            d0 = jnp.exp(jnp.where(causal, c0v - cumt[la:la + 1, :], -jnp.inf))
            d1 = jnp.exp(jnp.where(causal, c1v - cumt[lb:lb + 1, :], -jnp.inf))
            mm = jnp.concatenate([(cbm * d0).astype(BF16), (cbm * d1).astype(BF16)], axis=1)
            cols = slice(j0 * p, (j0 + 2) * p)
            xdt = act_ref[:, cols] * jnp.where(first, dtv[:, la:la + 1], dtv[:, lb:lb + 1])
            bd = jnp.concatenate([jnp.where(first, xdt, 0.0).astype(BF16),
                                  jnp.where(first, 0.0, xdt).astype(BF16)], axis=0)
            ydiag = jnp.dot(mm, bd, preferred_element_type=F32)
            cum_pair = jnp.where(first, c0v, c1v)
            y_ref[:, cols] = ydiag + yoff[:, 2 * pr * p:(2 * pr + 2) * p] * jnp.exp(cum_pair)
            last_pair = jnp.where(first[0:1, :], cumv[L - 1:L, la:la + 1], cumv[L - 1:L, lb:lb + 1])
            xd_parts.append((xdt * jnp.exp(last_pair - cum_pair)).astype(BF16))
        xd = jnp.concatenate(xd_parts, axis=1)
        snew = lax.dot_general(xd, bg, _TN, preferred_element_type=F32)
        for jj in range(hpg):
            j = g * hpg + jj
            cd = jnp.exp(cumv[L - 1:L, hoff + j:hoff + j + 1])
            rows = slice(j * p, (j + 1) * p)
            st_ref[rows, :] = st_ref[rows, :] * cd + snew[jj * p:(jj + 1) * p, :]

    ssq = jnp.zeros((L, 1), F32)
    for c0 in range(0, ds, slab):
        sl = slice(c0, c0 + slab)
        y = (y_ref[:, sl] + dsk_ref[:, sl] * act_ref[:, sl]) * _silu(z_ref[:, sl])
        y_ref[:, sl] = y
        ssq = ssq + jnp.sum(y * y, axis=-1, keepdims=True)
    rinv = lax.rsqrt(ssq / ds + EPS)
    for c0 in range(0, ds, slab):
        sl = slice(c0, c0 + slab)
        o_ref[:, sl] = (y_ref[:, sl] * rinv * g_ref[:, sl]).astype(o_ref.dtype)


def _ssd_prompt(xbc, z, dt, cs, cst, conv_w, conv_b, dskip_row, g_ssd, batch, seq, dims):
    ds, n_groups, n_state, n_heads, p, hoff, kw = dims
    m, dc = xbc.shape
    L = SSD_CHUNK
    nc = seq // L
    row = lambda b, c: (b * nc + c, 0)
    body = functools.partial(_ssd_body, ds=ds, n_groups=n_groups, n_state=n_state, n_heads=n_heads,
                             p=p, hoff=hoff, kw=kw)
    return pl.pallas_call(
        body,
        grid=(batch, nc),
        in_specs=[
            pl.BlockSpec((L, dc), row),
            pl.BlockSpec((L, ds), row),
            pl.BlockSpec((L, LANES), row),
            pl.BlockSpec((L, LANES), row),
            pl.BlockSpec((None, LANES, L), lambda b, c: (b, 0, c)),
            _resident((kw, dc)), _resident((1, dc)), _resident((1, ds)), _resident((1, ds)),
        ],
        out_specs=[pl.BlockSpec((L, ds), row),
                   pl.BlockSpec((None, n_heads * p, n_state), lambda b, c: (b, 0, 0))],
        out_shape=[jax.ShapeDtypeStruct((m, ds), BF16),
                   jax.ShapeDtypeStruct((batch, n_heads * p, n_state), F32)],
        scratch_shapes=[pltpu.VMEM((L + SUBLANES, dc), F32), pltpu.VMEM((L, dc), F32), pltpu.VMEM((L, ds), F32)],
        compiler_params=_params(("arbitrary", "arbitrary")),
        name="ssd_prompt",
    )(xbc, z, dt, cs, cst, conv_w, conv_b, dskip_row, g_ssd)


def _outproj_body(oa_ref, os_ref, x_ref, wa_ref, ws_ref, g_ref, x1_ref, hf_ref, *, tn):
    d = x_ref.shape[1]
    oa, osm = oa_ref[...], os_ref[...]
    for n0 in range(0, d, tn):
        sl = slice(n0, n0 + tn)
        mix = jnp.dot(oa, wa_ref[:, sl], preferred_element_type=F32) + jnp.dot(osm, ws_ref[:, sl], preferred_element_type=F32)
        x1_ref[:, sl] = x_ref[:, sl] + mix
    x1 = x1_ref[...]
    hf_ref[...] = (x1 * lax.rsqrt(jnp.mean(x1 * x1, axis=-1, keepdims=True) + EPS) * g_ref[...]).astype(hf_ref.dtype)


def _outproj(oa, osm, x, wa, ws, g_ffn, tm, name):
    m, d = x.shape
    row = lambda i: (i, 0)
    return pl.pallas_call(
        functools.partial(_outproj_body, tn=512),
        grid=(m // tm,),
        in_specs=[pl.BlockSpec((tm, oa.shape[1]), row), pl.BlockSpec((tm, osm.shape[1]), row),
                  pl.BlockSpec((tm, d), row), _resident(wa.shape), _resident(ws.shape), _resident((1, d))],
        out_specs=[pl.BlockSpec((tm, d), row)] * 2,
        out_shape=[jax.ShapeDtypeStruct((m, d), F32), jax.ShapeDtypeStruct((m, d), BF16)],
        compiler_params=_params(("arbitrary",)),
        name=name,
    )(oa, osm, x, wa, ws, g_ffn)


def _ffn_body(hf_ref, x1_ref, wg_ref, wu_ref, wd_ref, g_ref, o_ref):
    f = pl.program_id(1)
    hf = hf_ref[...]
    gate = jnp.dot(hf, wg_ref[...], preferred_element_type=F32)
    up = jnp.dot(hf, wu_ref[...], preferred_element_type=F32)
    part = jnp.dot((_silu(gate) * up).astype(BF16), wd_ref[...], preferred_element_type=F32)

    @pl.when(f == 0)
    def _():
        o_ref[...] = x1_ref[...] + part

    @pl.when(f > 0)
    def _():
        o_ref[...] += part

    @pl.when(f == pl.num_programs(1) - 1)
    def _():
        y = o_ref[...]
        o_ref[...] = y * lax.rsqrt(jnp.mean(y * y, axis=-1, keepdims=True) + EPS) * g_ref[...]


def _ffn(hf, x1, wg, wu, wd, g_final, tm, tf, name):
    m, d = x1.shape
    dff = wg.shape[1]
    return pl.pallas_call(
        _ffn_body,
        grid=(m // tm, dff // tf),
        in_specs=[pl.BlockSpec((tm, d), lambda i, f: (i, 0)), pl.BlockSpec((tm, d), lambda i, f: (i, 0)),
                  pl.BlockSpec((d, tf), lambda i, f: (0, f)), pl.BlockSpec((d, tf), lambda i, f: (0, f)),
                  pl.BlockSpec((tf, d), lambda i, f: (f, 0)), _resident((1, d))],
        out_specs=pl.BlockSpec((tm, d), lambda i, f: (i, 0)),
        out_shape=jax.ShapeDtypeStruct((m, d), F32),
        compiler_params=_params(("arbitrary", "arbitrary")),
        name=name,
    )(hf, x1, wg, wu, wd, g_final)


def _decode_body(pt_ref, q_ref, kn_ref, vn_ref, lfn_ref, g_ref, u_ref, vm_ref, slt_ref,
                 ck_hbm, cv_hbm, clf_hbm, o_ref,
                 kbuf, vbuf, lfbuf, sem, m_ref, l_ref, acc_ref, carry_ref,
                 *, ppc, nch, n_heads, dh, rh, scale):
    b = pl.program_id(0)
    c = pl.program_id(1)
    t = b * nch + c
    slot = lax.rem(t, 2)
    nsteps = pl.num_programs(0) * nch
    lf_rows = rh // LANES

    def copies(tt, sl_):
        out = []
        for j in range(ppc):
            pid = pt_ref[tt * ppc + j]
            out.append(pltpu.make_async_copy(ck_hbm.at[pid], kbuf.at[sl_, pl.ds(j * rh, rh), :], sem.at[0, sl_]))
            out.append(pltpu.make_async_copy(cv_hbm.at[pid], vbuf.at[sl_, pl.ds(j * rh, rh), :], sem.at[1, sl_]))
            out.append(pltpu.make_async_copy(clf_hbm.at[pid], lfbuf.at[sl_, pl.ds(j * lf_rows, lf_rows), :], sem.at[2, sl_]))
        return out

    @pl.when(t == 0)
    def _():
        for cp in copies(0, 0):
            cp.start()

    @pl.when(t + 1 < nsteps)
    def _():
        for cp in copies(t + 1, 1 - slot):
            cp.start()

    for cp in copies(t, slot):
        cp.wait()

    @pl.when(c == 0)
    def _():
        m_ref[...] = jnp.full(m_ref.shape, -jnp.inf, F32)
        l_ref[...] = jnp.zeros(l_ref.shape, F32)
        acc_ref[...] = jnp.zeros(acc_ref.shape, F32)
        carry_ref[...] = jnp.zeros(carry_ref.shape, F32)

    q = q_ref[...] * scale
    lane = lax.broadcasted_iota(jnp.int32, (n_heads, LANES), 1)
    head_match = jnp.bitwise_and(lane, n_heads - 1) == lax.broadcasted_iota(jnp.int32, (n_heads, LANES), 0)

    def update(s, valid, v_rows):
        s = jnp.where(valid, s, -jnp.inf)
        m_old = m_ref[...]
        m_new = jnp.maximum(m_old, jnp.max(s, axis=-1, keepdims=True))
        alpha = jnp.exp(m_old - m_new)
        pr = jnp.exp(s - m_new)
        l_ref[...] = alpha * l_ref[...] + jnp.sum(pr, axis=-1, keepdims=True)
        acc_ref[...] = alpha * acc_ref[...] + jnp.dot(pr, v_rows, preferred_element_type=F32)
        m_ref[...] = m_new

    lfv = lfbuf[slot]
    nrow = ppc * lf_rows
    within = _dot_r01(lfv, u_ref[...])
    rowtot = _dot_r01(lfv, vm_ref[...])
    offs = _dot_l01(slt_ref[...], rowtot)
    carry = carry_ref[...]
    cs = within + offs + carry
    carry_ref[...] = carry + offs[nrow - 1:nrow, :] + rowtot[nrow - 1:nrow, :]

    s = lax.dot_general(q, kbuf[slot], _NT, preferred_element_type=F32)
    bias = jnp.concatenate([jnp.broadcast_to(cs[i:i + 1, :], (n_heads, LANES)) for i in range(nrow)], axis=1)
    valid = jnp.concatenate([head_match] * nrow, axis=1)
    update(s - bias, valid, vbuf[slot])

    @pl.when(c == nch - 1)
    def _():
        zpad = jnp.zeros((LANES - n_heads, dh), F32)
        k_self = jnp.concatenate([zpad, kn_ref[...]], axis=0)
        v_self = jnp.concatenate([zpad, vn_ref[...]], axis=0)
        s_self = lax.dot_general(q, k_self, _NT, preferred_element_type=F32)
        c_t = carry_ref[...] + lfn_ref[...]
        update(s_self - c_t, jnp.logical_and(lane >= LANES - n_heads, head_match), v_self)
        o = acc_ref[...] / l_ref[...]
        ms = jnp.sum(jnp.sum(o * o, axis=-1, keepdims=True), axis=0, keepdims=True) / (n_heads * dh)
        o_ref[...] = (o * lax.rsqrt(ms + EPS) * g_ref[...]).astype(o_ref.dtype)


def _decode_consts(nrow, n_heads):
    li = np.arange(LANES)
    same_head = (li[:, None] % n_heads) == (li[None, :] % n_heads)
    u = same_head & (li[:, None] <= li[None, :])
    ri = np.arange(nrow)
    slt = ri[None, :] < ri[:, None]
    as_bf16 = lambda a: jnp.asarray(a.astype(np.float32), BF16)
    return as_bf16(u), as_bf16(same_head), as_bf16(slt)


def _decode_attention(page_table, q, k_new, v_new, lf_new, g_attn, cache_k, cache_v, cache_lf, ppc):
    nb, n_heads, dh = q.shape
    n_pages = page_table.shape[1]
    rh = cache_k.shape[1]
    nch = n_pages // ppc
    lf_rows = rh // LANES
    u, vm, slt = _decode_consts(ppc * lf_rows, n_heads)
    body = functools.partial(_decode_body, ppc=ppc, nch=nch, n_heads=n_heads, dh=dh, rh=rh, scale=dh ** -0.5)
    per_b = lambda b, c, pt: (b, 0, 0)
    const2 = lambda b, c, pt: (0, 0)
    any_spec = pl.BlockSpec(memory_space=pl.ANY)
    grid_spec = pltpu.PrefetchScalarGridSpec(
        num_scalar_prefetch=1,
        grid=(nb, nch),
        in_specs=[
            pl.BlockSpec((None, n_heads, dh), per_b),
            pl.BlockSpec((None, n_heads, dh), per_b),
            pl.BlockSpec((None, n_heads, dh), per_b),
            pl.BlockSpec((None, 1, LANES), per_b),
            pl.BlockSpec((n_heads, dh), const2),
            pl.BlockSpec(u.shape, const2), pl.BlockSpec(vm.shape, const2), pl.BlockSpec(slt.shape, const2),
            any_spec, any_spec, any_spec,
        ],
        out_specs=pl.BlockSpec((None, n_heads, dh), per_b),
        scratch_shapes=[
            pltpu.VMEM((2, ppc * rh, dh), F32), pltpu.VMEM((2, ppc * rh, dh), F32),
            pltpu.VMEM((2, ppc * lf_rows, LANES), F32),
            pltpu.SemaphoreType.DMA((3, 2)),
            pltpu.VMEM((n_heads, 1), F32), pltpu.VMEM((n_heads, 1), F32), pltpu.VMEM((n_heads, dh), F32),
            pltpu.VMEM((1, LANES), F32),
        ],
    )
    return pl.pallas_call(
        body,
        grid_spec=grid_spec,
        out_shape=jax.ShapeDtypeStruct((nb, n_heads, dh), BF16),
        compiler_params=_params(("arbitrary", "arbitrary")),
        name="paged_decode_attention",
    )(page_table.reshape(-1), q, k_new, v_new, lf_new, g_attn, u, vm, slt, cache_k, cache_v, cache_lf)


def _ssd_step_body(xbc_ref, sc_ref, z_ref, dt_ref, st_ref, cw_ref, cb_ref, arow_ref, e_ref, dsk_ref, g_ref,
                   o_ref, sto_ref, sco_ref, *, ds, n_groups, n_state, n_heads, p, hoff, kw):
    xn = xbc_ref[...]
    a = cw_ref[0:1, :] * sc_ref[0:1, :]
    for i in range(1, kw - 1):
        a = a + cw_ref[i:i + 1, :] * sc_ref[i:i + 1, :]
    a = cb_ref[...] + (a + cw_ref[kw - 1:kw, :] * xn)
    act = _silu(a)
    for i in range(kw - 2):
        sco_ref[i:i + 1, :] = sc_ref[i + 1:i + 2, :]
    sco_ref[kw - 2:kw - 1, :] = xn

    dt = dt_ref[...]
    da = dt * arow_ref[...]
    dt_exp = _dot_r01(jnp.broadcast_to(dt, (SUBLANES, LANES)), e_ref[...])[0:1, :]
    decay = jnp.exp(da)
    xs = act[:, :ds]
    xdt = xs * dt_exp
    hpg = n_heads // n_groups
    gw = hpg * p
    eye = lax.broadcasted_iota(jnp.int32, (gw, gw), 0) == lax.broadcasted_iota(jnp.int32, (gw, gw), 1)
    y_parts = []
    for g in range(n_groups):
        bg = act[:, ds + g * n_state:ds + (g + 1) * n_state]
        cg = act[:, ds + (n_groups + g) * n_state:ds + (n_groups + g + 1) * n_state]
        diag = jnp.where(eye, jnp.broadcast_to(xdt[:, g * gw:(g + 1) * gw], (gw, gw)), 0.0).astype(BF16)
        outer = jnp.dot(diag, jnp.broadcast_to(bg, (gw, n_state)).astype(BF16), preferred_element_type=F32)
        for jj in range(hpg):
            j = g * hpg + jj
            rows = slice(j * p, (j + 1) * p)
            sto_ref[rows, :] = st_ref[rows, :] * decay[:, hoff + j:hoff + j + 1] + outer[jj * p:(jj + 1) * p, :]
        st_new = sto_ref[g * gw:(g + 1) * gw, :].astype(BF16)
        c8 = jnp.broadcast_to(cg, (SUBLANES, n_state)).astype(BF16)
        y_parts.append(lax.dot_general(c8, st_new, _NT, preferred_element_type=F32)[0:1, :])
    y = jnp.concatenate(y_parts, axis=1)
    y = (y + dsk_ref[...] * xs) * _silu(z_ref[...])
    o_ref[...] = (y * lax.rsqrt(jnp.mean(y * y, axis=-1, keepdims=True) + EPS) * g_ref[...]).astype(o_ref.dtype)


def _ssd_step(xbc, state_conv, z, dt, state, conv_w, conv_b, a_row, expand, dskip_row, g_ssd, dims):
    ds, n_groups, n_state, n_heads, p, hoff, kw = dims
    nb, _, dc = xbc.shape
    per_b = lambda b: (b, 0, 0)
    body = functools.partial(_ssd_step_body, ds=ds, n_groups=n_groups, n_state=n_state, n_heads=n_heads,
                             p=p, hoff=hoff, kw=kw)
    return pl.pallas_call(
        body,
        grid=(nb,),
        in_specs=[
            pl.BlockSpec((None, 1, dc), per_b), pl.BlockSpec((None, kw - 1, dc), per_b),
            pl.BlockSpec((None, 1, ds), per_b), pl.BlockSpec((None, 1, LANES), per_b),
            pl.BlockSpec((None, n_heads * p, n_state), per_b),
            _resident((kw, dc)), _resident((1, dc)), _resident((1, LANES)), _resident(expand.shape),
            _resident((1, ds)), _resident((1, ds)),
        ],
        out_specs=[pl.BlockSpec((None, 1, ds), per_b), pl.BlockSpec((None, n_heads * p, n_state), per_b),
                   pl.BlockSpec((None, kw - 1, dc), per_b)],
        out_shape=[jax.ShapeDtypeStruct((nb, 1, ds), BF16), jax.ShapeDtypeStruct(state.shape, F32),
                   jax.ShapeDtypeStruct(state_conv.shape, F32)],
        compiler_params=_params(("arbitrary",)),
        name="ssd_step",
    )(xbc, state_conv, z, dt, state, conv_w, conv_b, a_row, expand, dskip_row, g_ssd)


def _pick_tile(m, pref):
    t = min(m, pref)
    while m % t:
        t //= 2
    return t


def kernel(x_prompt, x_sample, cache_k, cache_v, cache_logf, state_ssm, state_conv, page_table, g_mix, w_in, b_fgate, conv_w, conv_b, dt_bias, a_log, d_skip, g_attn, g_ssd, w_out, g_ffn, w_gate, w_up, w_down, g_final):
    bp, seq, d = x_prompt.shape
    bs = x_sample.shape[0]
    depth, n_pool, page, n_heads, dh = cache_k.shape
    da = n_heads * dh
    nh, p, n_state = state_ssm.shape[2:]
    ds = nh * p
    kw = conv_w.shape[1]
    dc = conv_w.shape[2]
    n_groups = (dc - ds) // (2 * n_state)
    hoff = n_heads
    assert depth == 1, "the FFN kernel applies the final rmsnorm, so it must be the last layer"
    assert x_sample.shape[1] == 1 and seq % SSD_CHUNK == 0 and n_heads + nh <= LANES - n_heads
    assert n_heads & (n_heads - 1) == 0
    assert (page * n_heads) % LANES == 0 and p * 2 == LANES and (nh // n_groups) % 2 == 0
    dims = (ds, n_groups, n_state, nh, p, hoff, kw)
    o_f, o_z, o_x, o_dt = 3 * da, 3 * da + n_heads, 3 * da + n_heads + ds, 3 * da + n_heads + ds + dc
    mp = bp * seq

    expand_np = np.zeros((LANES, ds), np.float32)
    for j in range(nh):
        expand_np[hoff + j, j * p:(j + 1) * p] = 1.0
    expand = jnp.asarray(expand_np, BF16)

    yp = x_prompt.reshape(mp, d)
    ys = x_sample.reshape(bs, d)
    outs = [[] for _ in range(10)]
    for l in range(depth):
        w = w_in[l]
        w_q, w_k, w_v = (w[:, i * da:(i + 1) * da].astype(BF16) for i in range(3))
        w_f = w[:, o_f:o_f + n_heads]
        w_fd = jnp.concatenate([w_f, w[:, o_dt:o_dt + nh], jnp.zeros((d, LANES - 2 * n_heads - nh), w.dtype), w_f],
                               axis=1).astype(BF16)
        w_z = w[:, o_z:o_z + ds].astype(BF16)
        w_x = w[:, o_x:o_x + dc].astype(BF16)
        w_oa = w_out[l][:da].astype(BF16)
        w_os = w_out[l][da:].astype(BF16)
        wg, wu, wd = w_gate[l].astype(BF16), w_up[l].astype(BF16), w_down[l].astype(BF16)
        zpad = jnp.zeros((LANES - 2 * n_heads - nh,), F32)
        bias_row = jnp.concatenate([b_fgate[l], dt_bias[l], zpad, b_fgate[l]]).reshape(1, LANES)
        a_row = jnp.concatenate([jnp.zeros((n_heads,), F32), -jnp.exp(a_log[l].astype(F32)),
                                 jnp.zeros((LANES - n_heads - nh,), F32)]).reshape(1, LANES)
        dskip_row = jnp.repeat(d_skip[l].astype(F32), p).reshape(1, ds)
        gm, ga, gs, gf = (g_mix[l].reshape(1, d), g_attn[l].reshape(1, da), g_ssd[l].reshape(1, ds),
                          g_ffn[l].reshape(1, d))
        gfin = g_final.reshape(1, d)
        cw, cb = conv_w[l].astype(F32), conv_b[l].reshape(1, dc).astype(F32)
        proj_a = dict(weights=[w_q, w_k, w_v, w_fd], out_map=[[0], [1, 3], [2, 4], [5]],
                      out_dtypes=[BF16, F32, F32, BF16, BF16, F32])
        proj_b = dict(weights=[w_z, w_x], out_map=[[0], [1]], out_dtypes=[F32, F32])

        tm = _pick_tile(mp, 512)
        q, k, v, kb, vb, fd = _norm_proj(yp, gm, tm=tm, name="in_proj_qkv", **proj_a)
        z, xbc = _norm_proj(yp, gm, tm=_pick_tile(mp, 256), name="in_proj_ssd", **proj_b)
        lf, dt, cs, cst = _gates(fd, bias_row, a_row, bp, seq, n_heads)
        oa = _attention(q, kb, vb, cst, ga, bp, seq, n_heads, dh, tq=_pick_tile(seq, 256))
        osm, st_p = _ssd_prompt(xbc, z, dt, cs, cst, cw, cb, dskip_row, gs, bp, seq, dims)
        x1, hf = _outproj(oa, osm, yp, w_oa, w_os, gf, tm, "out_proj")
        dff = wg.shape[1]
        tf = _pick_tile(dff, 512)
        yp_new = _ffn(hf, x1, wg, wu, wd, gfin, tm, tf, "ffn")

        qs, k_s, v_s, _, _, fds = _norm_proj(ys, gm, tm=bs, name="in_proj_qkv_s", **proj_a)
        zs, xbcs = _norm_proj(ys, gm, tm=bs, name="in_proj_ssd_s", **proj_b)
        lfs, dts = _sgates(fds, bias_row)
        ck = cache_k[l].reshape(n_pool, page * n_heads, dh)
        cv = cache_v[l].reshape(n_pool, page * n_heads, dh)
        clf = cache_logf[l].astype(F32).reshape(n_pool, page * n_heads // LANES, LANES)
        n_pages = page_table.shape[1]
        oas = _decode_attention(page_table, qs.astype(F32).reshape(bs, n_heads, dh),
                                k_s.reshape(bs, n_heads, dh), v_s.reshape(bs, n_heads, dh),
                                lfs.reshape(bs, 1, LANES), ga.reshape(n_heads, dh), ck, cv, clf,
                                ppc=_pick_tile(n_pages, 16))
        oss, st_s, conv_s = _ssd_step(xbcs.reshape(bs, 1, dc), state_conv[l].astype(F32), zs.reshape(bs, 1, ds),
                                      dts.reshape(bs, 1, LANES), state_ssm[l].astype(F32).reshape(bs, nh * p, n_state),
                                      cw, cb, a_row, expand, dskip_row, gs, dims)
        x1s, hfs = _outproj(oas.reshape(bs, da), oss.reshape(bs, ds), ys, w_oa, w_os, gf, bs, "out_proj_s")
        ys_new = _ffn(hfs, x1s, wg, wu, wd, gfin, bs, tf, "ffn_s")

        yp, ys = yp_new, ys_new
        new = [k.reshape(bp, seq, n_heads, dh), v.reshape(bp, seq, n_heads, dh),
               lf[:, :n_heads].reshape(bp, seq, n_heads),
               k_s.reshape(bs, 1, n_heads, dh), v_s.reshape(bs, 1, n_heads, dh),
               lfs[:, :n_heads].reshape(bs, 1, n_heads),
               st_p.reshape(bp, nh, p, n_state), xbc.reshape(bp, seq, dc)[:, seq - kw + 1:],
               st_s.reshape(bs, nh, p, n_state), conv_s]
        for acc, val in zip(outs, new):
            acc.append(val)

    return (yp.reshape(bp, seq, d), ys.reshape(bs, 1, d)) + tuple(jnp.stack(o) for o in outs)
```

```python
import functools

import numpy as np
import jax
import jax.numpy as jnp
from jax import lax
from jax.experimental import pallas as pl
from jax.experimental.pallas import tpu as pltpu

F32 = jnp.float32
BF16 = jnp.bfloat16
EPS = 1e-6
LOG2E = 1.4426950408889634
LANES = 128
SUBLANES = 8
SSD_CHUNK = 128
VMEM_LIMIT_MB = 56
FFN_SUB = 256

_NT = (((1,), (1,)), ((), ()))
_TN = (((0,), (0,)), ((), ()))


def _params(sem):
    return pltpu.CompilerParams(dimension_semantics=sem, vmem_limit_bytes=VMEM_LIMIT_MB * 2**20)


def _resident(shape):
    nd = len(shape)
    return pl.BlockSpec(shape, lambda *_: (0,) * nd, pipeline_mode=pl.Buffered(1))


def _silu(x):
    return x / (1.0 + jnp.exp(-x))


def _split3(x):
    hi = x.astype(BF16)
    r1 = x - hi.astype(F32)
    mid = r1.astype(BF16)
    lo = (r1 - mid.astype(F32)).astype(BF16)
    return hi, mid, lo


def _dot_r01(a, b01):
    hi, mid, lo = _split3(a)
    d = functools.partial(jnp.dot, preferred_element_type=F32)
    return d(hi, b01) + d(mid, b01) + d(lo, b01)


def _dot_l01(a01, b):
    hi, mid, lo = _split3(b)
    d = functools.partial(jnp.dot, preferred_element_type=F32)
    return d(a01, hi) + d(a01, mid) + d(a01, lo)


def _norm_proj_body(x_ref, g_ref, *refs, n_w, out_map, out_scale, tn):
    w_refs, o_refs = refs[:n_w], refs[n_w:]
    x = x_ref[...]
    h = (x * lax.rsqrt(jnp.mean(x * x, axis=-1, keepdims=True) + EPS) * g_ref[...]).astype(BF16)
    for wi, w_ref in enumerate(w_refs):
        n = w_ref.shape[1]
        for n0 in range(0, n, tn):
            n1 = min(n0 + tn, n)
            r = jnp.dot(h, w_ref[:, n0:n1], preferred_element_type=F32)
            for oi in out_map[wi]:
                o_refs[oi][:, n0:n1] = (r * out_scale[oi] if oi in out_scale else r).astype(o_refs[oi].dtype)


def _norm_proj(x, g, weights, out_map, out_dtypes, tm, name, out_scale=None):
    m, d = x.shape
    widths = {}
    for wi, ois in enumerate(out_map):
        for oi in ois:
            widths[oi] = weights[wi].shape[1]
    n_out = len(out_dtypes)
    body = functools.partial(_norm_proj_body, n_w=len(weights), out_map=out_map, out_scale=out_scale or {}, tn=512)
    return pl.pallas_call(
        body,
        grid=(m // tm,),
        in_specs=[pl.BlockSpec((tm, d), lambda i: (i, 0)), _resident((1, d))]
        + [_resident(w.shape) for w in weights],
        out_specs=[pl.BlockSpec((tm, widths[oi]), lambda i: (i, 0)) for oi in range(n_out)],
        out_shape=[jax.ShapeDtypeStruct((m, widths[oi]), out_dtypes[oi]) for oi in range(n_out)],
        compiler_params=_params(("arbitrary",)),
        name=name,
    )(x, g, *weights)


def _gates_body(fd_ref, bias_ref, arow_ref, lf_ref, dt_ref, cs_ref, cst_ref, *, n_heads):
    t = fd_ref.shape[0]
    L = SSD_CHUNK
    lane = lax.broadcasted_iota(jnp.int32, (L, LANES), 1)
    tri = (lax.broadcasted_iota(jnp.int32, (L, L), 0) >= lax.broadcasted_iota(jnp.int32, (L, L), 1)).astype(BF16)
    is_f = lane < n_heads
    carry = jnp.zeros((1, LANES), F32)
    for c in range(t // L):
        rows = slice(c * L, (c + 1) * L)
        v = fd_ref[rows, :] + bias_ref[...]
        sp = jnp.log1p(jnp.exp(-jnp.abs(v)))
        lf = jnp.minimum(v, 0.0) - sp
        dt = jnp.maximum(v, 0.0) + sp
        zc = jnp.where(is_f, lf, dt * arow_ref[...])
        cs = _dot_l01(tri, zc) + jnp.where(is_f, carry, 0.0)
        carry = cs[L - 1:L, :]
        lf_ref[rows, :] = lf
        dt_ref[rows, :] = dt
        cs_ref[rows, :] = cs
        cst_ref[:, rows] = cs.T


def _gates(fd, bias_row, a_row, batch, seq, n_heads):
    m = fd.shape[0]
    row = lambda b: (b, 0)
    return pl.pallas_call(
        functools.partial(_gates_body, n_heads=n_heads),
        grid=(batch,),
        in_specs=[pl.BlockSpec((seq, LANES), row), _resident((1, LANES)), _resident((1, LANES))],
        out_specs=[pl.BlockSpec((seq, LANES), row)] * 3 + [pl.BlockSpec((None, LANES, seq), lambda b: (b, 0, 0))],
        out_shape=[jax.ShapeDtypeStruct((m, LANES), F32)] * 3 + [jax.ShapeDtypeStruct((batch, LANES, seq), F32)],
        compiler_params=_params(("arbitrary",)),
        name="gates",
    )(fd, bias_row, a_row)


def _sgates_body(fd_ref, bias_ref, lf_ref, dt_ref):
    v = fd_ref[...] + bias_ref[...]
    sp = jnp.log1p(jnp.exp(-jnp.abs(v)))
    lf_ref[...] = jnp.minimum(v, 0.0) - sp
    dt_ref[...] = jnp.maximum(v, 0.0) + sp


def _sgates(fd, bias_row):
    return pl.pallas_call(
        _sgates_body,
        out_shape=[jax.ShapeDtypeStruct(fd.shape, F32)] * 2,
        name="sample_gates",
    )(fd, bias_row)


def _attn_body(q_ref, k_ref, v_ref, ct_ref, g_ref, o_ref, m_ref, l_ref, acc_ref, *, tq, n_heads, dh):
    qi = pl.program_id(1)
    causal = lax.broadcasted_iota(jnp.int32, (tq, tq), 0) >= lax.broadcasted_iota(jnp.int32, (tq, tq), 1)
    m_ref[...] = jnp.full(m_ref.shape, -jnp.inf, F32)
    l_ref[...] = jnp.zeros(l_ref.shape, F32)
    acc_ref[...] = jnp.zeros(acc_ref.shape, F32)

    def kv_block(kj, masked):
        k0 = pl.multiple_of(kj * tq, tq)
        for h in range(n_heads):
            sl = slice(h * dh, (h + 1) * dh)
            s = lax.dot_general(q_ref[:, sl], k_ref[pl.ds(k0, tq), sl], _NT, preferred_element_type=F32)
            s = s - ct_ref[h:h + 1, pl.ds(k0, tq)] * LOG2E
            if masked:
                s = jnp.where(causal, s, -jnp.inf)
            m_old = m_ref[h]
            m_new = jnp.maximum(m_old, jnp.max(s, axis=-1, keepdims=True))
            alpha = jnp.exp2(m_old - m_new)
            p = jnp.exp2(s - jnp.concatenate([m_new] * (tq // LANES), axis=1))
            psum = p[:, :LANES]
            for c0 in range(LANES, tq, LANES):
                psum = psum + p[:, c0:c0 + LANES]
            l_ref[h] = alpha * l_ref[h] + psum
            acc_ref[:, sl] = alpha * acc_ref[:, sl] + jnp.dot(p.astype(BF16), v_ref[pl.ds(k0, tq), sl],
                                                              preferred_element_type=F32)
            m_ref[h] = m_new

    def loop_body(kj, carry):
        kv_block(kj, False)
        return carry

    lax.fori_loop(0, qi, loop_body, 0)
    kv_block(qi, True)
    for h in range(n_heads):
        sl = slice(h * dh, (h + 1) * dh)
        acc_ref[:, sl] = acc_ref[:, sl] / jnp.sum(l_ref[h], axis=-1, keepdims=True)
    o = acc_ref[...]
    o_ref[...] = (o * lax.rsqrt(jnp.mean(o * o, axis=-1, keepdims=True) + EPS) * g_ref[...]).astype(o_ref.dtype)


def _attention(q, kb, vb, cst, g_attn, batch, seq, n_heads, dh, tq):
    m, da = q.shape
    nq = seq // tq
    body = functools.partial(_attn_body, tq=tq, n_heads=n_heads, dh=dh)
    return pl.pallas_call(
        body,
        grid=(batch, nq),
        in_specs=[
            pl.BlockSpec((tq, da), lambda b, i: (b * nq + i, 0)),
            pl.BlockSpec((seq, da), lambda b, i: (b, 0)),
            pl.BlockSpec((seq, da), lambda b, i: (b, 0)),
            pl.BlockSpec((None, SUBLANES, seq), lambda b, i: (b, 0, 0)),
            _resident((1, da)),
        ],
        out_specs=pl.BlockSpec((tq, da), lambda b, i: (b * nq + i, 0)),
        out_shape=jax.ShapeDtypeStruct((m, da), BF16),
        scratch_shapes=[pltpu.VMEM((n_heads, tq, LANES), F32), pltpu.VMEM((n_heads, tq, LANES), F32),
                        pltpu.VMEM((tq, da), F32)],
        compiler_params=_params(("arbitrary", "arbitrary")),
        name="fox_attention",
    )(q, kb, vb, cst, g_attn)


def _ssd_body(xbc_ref, z_ref, dt_ref, cs_ref, cst_ref, cw_ref, cb_ref, dsk_ref, g_ref,
              o_ref, st_ref, xp_ref, act_ref, y_ref, *, ds, n_groups, n_state, n_heads, p, hoff, kw):
    L = SSD_CHUNK
    dc = xbc_ref.shape[1]
    c = pl.program_id(1)
    pad = SUBLANES

    @pl.when(c == 0)
    def _():
        st_ref[...] = jnp.zeros_like(st_ref)
        xp_ref[:, 0:pad, :] = jnp.zeros((dc // LANES, pad, LANES), F32)

    slab = 512
    for t in range(dc // LANES):
        sl = slice(t * LANES, (t + 1) * LANES)
        xp_ref[t, pad:pad + L, :] = xbc_ref[:, sl]
        a = cw_ref[0:1, sl] * xp_ref[t, pad - kw + 1:pad - kw + 1 + L, :]
        for i in range(1, kw):
            a = a + cw_ref[i:i + 1, sl] * xp_ref[t, pad - kw + 1 + i:pad - kw + 1 + i + L, :]
        a = cb_ref[:, sl] + a
        act_ref[:, sl] = _silu(a)
        xp_ref[t, 0:pad, :] = xp_ref[t, L:L + pad, :]

    dtv = dt_ref[...]
    cumv = cs_ref[...]
    cumt = cst_ref[...]
    lane = lax.broadcasted_iota(jnp.int32, (L, LANES), 1)
    causal = lax.broadcasted_iota(jnp.int32, (L, L), 0) >= lax.broadcasted_iota(jnp.int32, (L, L), 1)
    first = lane < p
    hpg = n_heads // n_groups
    gw = hpg * p
    for g in range(n_groups):
        bg = act_ref[:, ds + g * n_state:ds + (g + 1) * n_state].astype(BF16)
        cg = act_ref[:, ds + (n_groups + g) * n_state:ds + (n_groups + g + 1) * n_state].astype(BF16)
        cbm = lax.dot_general(cg, bg, _NT, preferred_element_type=F32)
        st_g = st_ref[g * gw:(g + 1) * gw, :]
        yoff = lax.dot_general(cg, st_g.astype(BF16), _NT, preferred_element_type=F32)
        xd_parts = []
        for pr in range(hpg // 2):
            j0 = g * hpg + 2 * pr
            la, lb = hoff + j0, hoff + j0 + 1
            c0v, c1v = cumv[:, la:la + 1], cumv[:, lb:lb + 1]
            d0 = jnp.exp(jnp.where(causal, c0v - cumt[la:la + 1, :], -jnp.inf))
            d1 = jnp.exp(jnp.where(causal, c1v - cumt[lb:lb + 1, :], -jnp.inf))
            mm = jnp.concatenate([(cbm * d0).astype(BF16), (cbm * d1).astype(BF16)], axis=1)
            cols = slice(j0 * p, (j0 + 2) * p)
            xdt = act_ref[:, cols] * jnp.where(first, dtv[:, la:la + 1], dtv[:, lb:lb + 1])
            bd = jnp.concatenate([jnp.where(first, xdt, 0.0).astype(BF16),
                                  jnp.where(first, 0.0, xdt).astype(BF16)], axis=0)
            ydiag = jnp.dot(mm, bd, preferred_element_type=F32)
            cum_pair = jnp.where(first, c0v, c1v)
            y_ref[:, cols] = ydiag + yoff[:, 2 * pr * p:(2 * pr + 2) * p] * jnp.exp(cum_pair)
            last_pair = jnp.where(first[0:1, :], cumv[L - 1:L, la:la + 1], cumv[L - 1:L, lb:lb + 1])
            xd_parts.append((xdt * jnp.exp(last_pair - cum_pair)).astype(BF16))
        xd = jnp.concatenate(xd_parts, axis=1)
        snew = lax.dot_general(xd, bg, _TN, preferred_element_type=F32)
        for jj in range(hpg):
            j = g * hpg + jj
            cd = jnp.exp(cumv[L - 1:L, hoff + j:hoff + j + 1])
            rows = slice(j * p, (j + 1) * p)
            st_ref[rows, :] = st_ref[rows, :] * cd + snew[jj * p:(jj + 1) * p, :]

    ssq = jnp.zeros((L, 1), F32)
    for c0 in range(0, ds, slab):
        sl = slice(c0, c0 + slab)
        y = (y_ref[:, sl] + dsk_ref[:, sl] * act_ref[:, sl]) * _silu(z_ref[:, sl])
        y_ref[:, sl] = y
        ssq = ssq + jnp.sum(y * y, axis=-1, keepdims=True)
    rinv = lax.rsqrt(ssq / ds + EPS)
    for c0 in range(0, ds, slab):
        sl = slice(c0, c0 + slab)
        o_ref[:, sl] = (y_ref[:, sl] * rinv * g_ref[:, sl]).astype(o_ref.dtype)


def _ssd_prompt(xbc, z, dt, cs, cst, conv_w, conv_b, dskip_row, g_ssd, batch, seq, dims):
    ds, n_groups, n_state, n_heads, p, hoff, kw = dims
    m, dc = xbc.shape
    L = SSD_CHUNK
    nc = seq // L
    row = lambda b, c: (b * nc + c, 0)
    body = functools.partial(_ssd_body, ds=ds, n_groups=n_groups, n_state=n_state, n_heads=n_heads,
                             p=p, hoff=hoff, kw=kw)
    return pl.pallas_call(
        body,
        grid=(batch, nc),
        in_specs=[
            pl.BlockSpec((L, dc), row),
            pl.BlockSpec((L, ds), row),
            pl.BlockSpec((L, LANES), row),
            pl.BlockSpec((L, LANES), row),
            pl.BlockSpec((None, LANES, L), lambda b, c: (b, 0, c)),
            _resident((kw, dc)), _resident((1, dc)), _resident((1, ds)), _resident((1, ds)),
        ],
        out_specs=[pl.BlockSpec((L, ds), row),
                   pl.BlockSpec((None, n_heads * p, n_state), lambda b, c: (b, 0, 0))],
        out_shape=[jax.ShapeDtypeStruct((m, ds), BF16),
                   jax.ShapeDtypeStruct((batch, n_heads * p, n_state), F32)],
        scratch_shapes=[pltpu.VMEM((dc // LANES, L + SUBLANES, LANES), F32), pltpu.VMEM((L, dc), F32),
                        pltpu.VMEM((L, ds), F32)],
        compiler_params=_params(("arbitrary", "arbitrary")),
        name="ssd_prompt",
    )(xbc, z, dt, cs, cst, conv_w, conv_b, dskip_row, g_ssd)


def _outproj_body(oa_ref, os_ref, x_ref, wa_ref, ws_ref, g_ref, x1_ref, hf_ref, *, tn):
    d = x_ref.shape[1]
    oa, osm = oa_ref[...], os_ref[...]
    for n0 in range(0, d, tn):
        sl = slice(n0, n0 + tn)
        mix = jnp.dot(oa, wa_ref[:, sl], preferred_element_type=F32) + jnp.dot(osm, ws_ref[:, sl], preferred_element_type=F32)
        x1_ref[:, sl] = x_ref[:, sl] + mix
    x1 = x1_ref[...]
    hf_ref[...] = (x1 * lax.rsqrt(jnp.mean(x1 * x1, axis=-1, keepdims=True) + EPS) * g_ref[...]).astype(hf_ref.dtype)


def _outproj(oa, osm, x, wa, ws, g_ffn, tm, name):
    m, d = x.shape
    row = lambda i: (i, 0)
    return pl.pallas_call(
        functools.partial(_outproj_body, tn=512),
        grid=(m // tm,),
        in_specs=[pl.BlockSpec((tm, oa.shape[1]), row), pl.BlockSpec((tm, osm.shape[1]), row),
                  pl.BlockSpec((tm, d), row), _resident(wa.shape), _resident(ws.shape), _resident((1, d))],
        out_specs=[pl.BlockSpec((tm, d), row)] * 2,
        out_shape=[jax.ShapeDtypeStruct((m, d), F32), jax.ShapeDtypeStruct((m, d), BF16)],
        compiler_params=_params(("arbitrary",)),
        name=name,
    )(oa, osm, x, wa, ws, g_ffn)


def _ffn_body(hf_ref, x1_ref, wg_ref, wu_ref, wd_ref, g_ref, o_ref):
    f = pl.program_id(1)

    @pl.when(f == 0)
    def _():
        o_ref[...] = x1_ref[...]

    hf = hf_ref[...]
    tf = wg_ref.shape[1]
    acts = []
    for c0 in range(0, tf, FFN_SUB):
        sl = slice(c0, min(c0 + FFN_SUB, tf))
        gate = jnp.dot(hf, wg_ref[:, sl], preferred_element_type=F32)
        up = jnp.dot(hf, wu_ref[:, sl], preferred_element_type=F32)
        acts.append((_silu(gate) * up).astype(BF16))
    act = acts[0] if len(acts) == 1 else jnp.concatenate(acts, axis=1)
    part = jnp.dot(act, wd_ref[...], preferred_element_type=F32)

    o_ref[...] += part

    @pl.when(f == pl.num_programs(1) - 1)
    def _():
        y = o_ref[...]
        o_ref[...] = y * lax.rsqrt(jnp.mean(y * y, axis=-1, keepdims=True) + EPS) * g_ref[...]


def _ffn(hf, x1, wg, wu, wd, g_final, tm, tf, name):
    m, d = x1.shape
    dff = wg.shape[1]
    return pl.pallas_call(
        _ffn_body,
        grid=(m // tm, dff // tf),
        in_specs=[pl.BlockSpec((tm, d), lambda i, f: (i, 0)), pl.BlockSpec((tm, d), lambda i, f: (i, 0)),
                  pl.BlockSpec((d, tf), lambda i, f: (0, f)), pl.BlockSpec((d, tf), lambda i, f: (0, f)),
                  pl.BlockSpec((tf, d), lambda i, f: (f, 0)), _resident((1, d))],
        out_specs=pl.BlockSpec((tm, d), lambda i, f: (i, 0)),
        out_shape=jax.ShapeDtypeStruct((m, d), F32),
        compiler_params=_params(("arbitrary", "arbitrary")),
        name=name,
    )(hf, x1, wg, wu, wd, g_final)


def _decode_body(pt_ref, q_ref, kn_ref, vn_ref, lfn_ref, g_ref, u_ref, vm_ref, slt_ref,
                 ck_hbm, cv_hbm, clf_hbm, o_ref,
                 kbuf, vbuf, lfbuf, sem, m_ref, l_ref, acc_ref, carry_ref,
                 *, ppc, nch, n_heads, dh, rh):
    b = pl.program_id(0)
    c = pl.program_id(1)
    t = b * nch + c
    slot = lax.rem(t, 2)
    nsteps = pl.num_programs(0) * nch
    lf_rows = rh // LANES

    def copies(tt, sl_):
        out = []
        for j in range(ppc):
            pid = pt_ref[tt * ppc + j]
            out.append(pltpu.make_async_copy(ck_hbm.at[pid], kbuf.at[sl_, pl.ds(j * rh, rh), :], sem.at[0, sl_]))
            out.append(pltpu.make_async_copy(cv_hbm.at[pid], vbuf.at[sl_, pl.ds(j * rh, rh), :], sem.at[1, sl_]))
            out.append(pltpu.make_async_copy(clf_hbm.at[pid], lfbuf.at[sl_, pl.ds(j * lf_rows, lf_rows), :], sem.at[2, sl_]))
        return out

    @pl.when(t == 0)
    def _():
        for cp in copies(0, 0):
            cp.start()

    @pl.when(t + 1 < nsteps)
    def _():
        for cp in copies(t + 1, 1 - slot):
            cp.start()

    for cp in copies(t, slot):
        cp.wait()

    @pl.when(c == 0)
    def _():
        m_ref[...] = jnp.full(m_ref.shape, -jnp.inf, F32)
        l_ref[...] = jnp.zeros(l_ref.shape, F32)
        acc_ref[...] = jnp.zeros(acc_ref.shape, F32)
        carry_ref[...] = jnp.zeros(carry_ref.shape, F32)

    q = q_ref[...]
    lane = lax.broadcasted_iota(jnp.int32, (n_heads, LANES), 1)
    head_match = jnp.bitwise_and(lane, n_heads - 1) == lax.broadcasted_iota(jnp.int32, (n_heads, LANES), 0)

    def update(s, valid, v_rows):
        s = jnp.where(valid, s, -jnp.inf)
        m_old = m_ref[...]
        m_new = jnp.maximum(m_old, jnp.max(s, axis=-1, keepdims=True))
        alpha = jnp.exp(m_old - m_new)
        pr = jnp.exp(s - m_new)
        l_ref[...] = alpha * l_ref[...] + jnp.sum(pr, axis=-1, keepdims=True)
        acc_ref[...] = alpha * acc_ref[...] + jnp.dot(pr, v_rows, preferred_element_type=F32)
        m_ref[...] = m_new

    lfv = lfbuf[slot]
    nrow = ppc * lf_rows
    within = _dot_r01(lfv, u_ref[...])
    rowtot = _dot_r01(lfv, vm_ref[...])
    offs = _dot_l01(slt_ref[...], rowtot)
    carry = carry_ref[...]
    cs = within + offs + carry
    carry_ref[...] = carry + offs[nrow - 1:nrow, :] + rowtot[nrow - 1:nrow, :]

    s = lax.dot_general(q, kbuf[slot], _NT, preferred_element_type=F32)
    bias = jnp.concatenate([jnp.broadcast_to(cs[i:i + 1, :], (n_heads, LANES)) for i in range(nrow)], axis=1)
    valid = jnp.concatenate([head_match] * nrow, axis=1)
    update(s - bias, valid, vbuf[slot])

    @pl.when(c == nch - 1)
    def _():
        zpad = jnp.zeros((LANES - n_heads, dh), F32)
        k_self = jnp.concatenate([zpad, kn_ref[...]], axis=0)
        v_self = jnp.concatenate([zpad, vn_ref[...]], axis=0)
        s_self = lax.dot_general(q, k_self, _NT, preferred_element_type=F32)
        c_t = carry_ref[...] + lfn_ref[...]
        update(s_self - c_t, jnp.logical_and(lane >= LANES - n_heads, head_match), v_self)
        o = acc_ref[...] / l_ref[...]
        ms = jnp.sum(jnp.sum(o * o, axis=-1, keepdims=True), axis=0, keepdims=True) / (n_heads * dh)
        o_ref[...] = (o * lax.rsqrt(ms + EPS) * g_ref[...]).astype(o_ref.dtype)


def _decode_consts(nrow, n_heads):
    li = np.arange(LANES)
    same_head = (li[:, None] % n_heads) == (li[None, :] % n_heads)
    u = same_head & (li[:, None] <= li[None, :])
    ri = np.arange(nrow)
    slt = ri[None, :] < ri[:, None]
    as_bf16 = lambda a: jnp.asarray(a.astype(np.float32), BF16)
    return as_bf16(u), as_bf16(same_head), as_bf16(slt)


def _decode_attention(page_table, q, k_new, v_new, lf_new, g_attn, cache_k, cache_v, cache_lf, ppc):
    nb, n_heads, dh = q.shape
    n_pages = page_table.shape[1]
    rh = cache_k.shape[1]
    nch = n_pages // ppc
    lf_rows = rh // LANES
    u, vm, slt = _decode_consts(ppc * lf_rows, n_heads)
    body = functools.partial(_decode_body, ppc=ppc, nch=nch, n_heads=n_heads, dh=dh, rh=rh)
    per_b = lambda b, c, pt: (b, 0, 0)
    const2 = lambda b, c, pt: (0, 0)
    any_spec = pl.BlockSpec(memory_space=pl.ANY)
    grid_spec = pltpu.PrefetchScalarGridSpec(
        num_scalar_prefetch=1,
        grid=(nb, nch),
        in_specs=[
            pl.BlockSpec((None, n_heads, dh), per_b),
            pl.BlockSpec((None, n_heads, dh), per_b),
            pl.BlockSpec((None, n_heads, dh), per_b),
            pl.BlockSpec((None, 1, LANES), per_b),
            pl.BlockSpec((n_heads, dh), const2),
            pl.BlockSpec(u.shape, const2), pl.BlockSpec(vm.shape, const2), pl.BlockSpec(slt.shape, const2),
            any_spec, any_spec, any_spec,
        ],
        out_specs=pl.BlockSpec((None, n_heads, dh), per_b),
        scratch_shapes=[
            pltpu.VMEM((2, ppc * rh, dh), F32), pltpu.VMEM((2, ppc * rh, dh), F32),
            pltpu.VMEM((2, ppc * lf_rows, LANES), F32),
            pltpu.SemaphoreType.DMA((3, 2)),
            pltpu.VMEM((n_heads, 1), F32), pltpu.VMEM((n_heads, 1), F32), pltpu.VMEM((n_heads, dh), F32),
            pltpu.VMEM((1, LANES), F32),
        ],
    )
    return pl.pallas_call(
        body,
        grid_spec=grid_spec,
        out_shape=jax.ShapeDtypeStruct((nb, n_heads, dh), BF16),
        compiler_params=_params(("arbitrary", "arbitrary")),
        name="paged_decode_attention",
    )(page_table.reshape(-1), q, k_new, v_new, lf_new, g_attn, u, vm, slt, cache_k, cache_v, cache_lf)


def _ssd_step_body(xbc_ref, sc_ref, z_ref, dt_ref, st_ref, cw_ref, cb_ref, arow_ref, e_ref, dsk_ref, g_ref,
                   o_ref, sto_ref, sco_ref, *, ds, n_groups, n_state, n_heads, p, hoff, kw):
    xn = xbc_ref[...]
    a = cw_ref[0:1, :] * sc_ref[0:1, :]
    for i in range(1, kw - 1):
        a = a + cw_ref[i:i + 1, :] * sc_ref[i:i + 1, :]
    a = cb_ref[...] + (a + cw_ref[kw - 1:kw, :] * xn)
    act = _silu(a)
    for i in range(kw - 2):
        sco_ref[i:i + 1, :] = sc_ref[i + 1:i + 2, :]
    sco_ref[kw - 2:kw - 1, :] = xn

    dt = dt_ref[...]
    da = dt * arow_ref[...]
    dt_exp = _dot_r01(jnp.broadcast_to(dt, (SUBLANES, LANES)), e_ref[...])[0:1, :]
    decay = jnp.exp(da)
    xs = act[:, :ds]
    xdt = xs * dt_exp
    hpg = n_heads // n_groups
    gw = hpg * p
    eye = lax.broadcasted_iota(jnp.int32, (gw, gw), 0) == lax.broadcasted_iota(jnp.int32, (gw, gw), 1)
    y_parts = []
    for g in range(n_groups):
        bg = act[:, ds + g * n_state:ds + (g + 1) * n_state]
        cg = act[:, ds + (n_groups + g) * n_state:ds + (n_groups + g + 1) * n_state]
        diag = jnp.where(eye, jnp.broadcast_to(xdt[:, g * gw:(g + 1) * gw], (gw, gw)), 0.0).astype(BF16)
        outer = jnp.dot(diag, jnp.broadcast_to(bg, (gw, n_state)).astype(BF16), preferred_element_type=F32)
        for jj in range(hpg):
            j = g * hpg + jj
            rows = slice(j * p, (j + 1) * p)
            sto_ref[rows, :] = st_ref[rows, :] * decay[:, hoff + j:hoff + j + 1] + outer[jj * p:(jj + 1) * p, :]
        st_new = sto_ref[g * gw:(g + 1) * gw, :].astype(BF16)
        c8 = jnp.broadcast_to(cg, (SUBLANES, n_state)).astype(BF16)
        y_parts.append(lax.dot_general(c8, st_new, _NT, preferred_element_type=F32)[0:1, :])
    y = jnp.concatenate(y_parts, axis=1)
    y = (y + dsk_ref[...] * xs) * _silu(z_ref[...])
    o_ref[...] = (y * lax.rsqrt(jnp.mean(y * y, axis=-1, keepdims=True) + EPS) * g_ref[...]).astype(o_ref.dtype)


def _ssd_step(xbc, state_conv, z, dt, state, conv_w, conv_b, a_row, expand, dskip_row, g_ssd, dims):
    ds, n_groups, n_state, n_heads, p, hoff, kw = dims
    nb, _, dc = xbc.shape
    per_b = lambda b: (b, 0, 0)
    body = functools.partial(_ssd_step_body, ds=ds, n_groups=n_groups, n_state=n_state, n_heads=n_heads,
                             p=p, hoff=hoff, kw=kw)
    return pl.pallas_call(
        body,
        grid=(nb,),
        in_specs=[
            pl.BlockSpec((None, 1, dc), per_b), pl.BlockSpec((None, kw - 1, dc), per_b),
            pl.BlockSpec((None, 1, ds), per_b), pl.BlockSpec((None, 1, LANES), per_b),
            pl.BlockSpec((None, n_heads * p, n_state), per_b),
            _resident((kw, dc)), _resident((1, dc)), _resident((1, LANES)), _resident(expand.shape),
            _resident((1, ds)), _resident((1, ds)),
        ],
        out_specs=[pl.BlockSpec((None, 1, ds), per_b), pl.BlockSpec((None, n_heads * p, n_state), per_b),
                   pl.BlockSpec((None, kw - 1, dc), per_b)],
        out_shape=[jax.ShapeDtypeStruct((nb, 1, ds), BF16), jax.ShapeDtypeStruct(state.shape, F32),
                   jax.ShapeDtypeStruct(state_conv.shape, F32)],
        compiler_params=_params(("arbitrary",)),
        name="ssd_step",
    )(xbc, state_conv, z, dt, state, conv_w, conv_b, a_row, expand, dskip_row, g_ssd)


def _pick_tile(m, pref):
    t = min(m, pref)
    while m % t:
        t //= 2
    return t


def kernel(x_prompt, x_sample, cache_k, cache_v, cache_logf, state_ssm, state_conv, page_table, g_mix, w_in, b_fgate, conv_w, conv_b, dt_bias, a_log, d_skip, g_attn, g_ssd, w_out, g_ffn, w_gate, w_up, w_down, g_final):
    bp, seq, d = x_prompt.shape
    bs = x_sample.shape[0]
    depth, n_pool, page, n_heads, dh = cache_k.shape
    da = n_heads * dh
    nh, p, n_state = state_ssm.shape[2:]
    ds = nh * p
    kw = conv_w.shape[1]
    dc = conv_w.shape[2]
    n_groups = (dc - ds) // (2 * n_state)
    hoff = n_heads
    assert depth == 1, "the FFN kernel applies the final rmsnorm, so it must be the last layer"
    assert x_sample.shape[1] == 1 and seq % SSD_CHUNK == 0 and n_heads + nh <= LANES - n_heads
    assert n_heads & (n_heads - 1) == 0
    assert (page * n_heads) % LANES == 0 and p * 2 == LANES and (nh // n_groups) % 2 == 0
    dims = (ds, n_groups, n_state, nh, p, hoff, kw)
    o_f, o_z, o_x, o_dt = 3 * da, 3 * da + n_heads, 3 * da + n_heads + ds, 3 * da + n_heads + ds + dc
    mp = bp * seq

    expand_np = np.zeros((LANES, ds), np.float32)
    for j in range(nh):
        expand_np[hoff + j, j * p:(j + 1) * p] = 1.0
    expand = jnp.asarray(expand_np, BF16)

    yp = x_prompt.reshape(mp, d)
    ys = x_sample.reshape(bs, d)
    outs = [[] for _ in range(10)]
    for l in range(depth):
        w = w_in[l]
        w_q, w_k, w_v = (w[:, i * da:(i + 1) * da].astype(BF16) for i in range(3))
        w_f = w[:, o_f:o_f + n_heads]
        w_fd = jnp.concatenate([w_f, w[:, o_dt:o_dt + nh], jnp.zeros((d, LANES - 2 * n_heads - nh), w.dtype), w_f],
                               axis=1).astype(BF16)
        w_z = w[:, o_z:o_z + ds].astype(BF16)
        w_x = w[:, o_x:o_x + dc].astype(BF16)
        w_oa = w_out[l][:da].astype(BF16)
        w_os = w_out[l][da:].astype(BF16)
        wg, wu, wd = w_gate[l].astype(BF16), w_up[l].astype(BF16), w_down[l].astype(BF16)
        zpad = jnp.zeros((LANES - 2 * n_heads - nh,), F32)
        bias_row = jnp.concatenate([b_fgate[l], dt_bias[l], zpad, b_fgate[l]]).reshape(1, LANES)
        a_row = jnp.concatenate([jnp.zeros((n_heads,), F32), -jnp.exp(a_log[l].astype(F32)),
                                 jnp.zeros((LANES - n_heads - nh,), F32)]).reshape(1, LANES)
        dskip_row = jnp.repeat(d_skip[l].astype(F32), p).reshape(1, ds)
        gm, ga, gs, gf = (g_mix[l].reshape(1, d), g_attn[l].reshape(1, da), g_ssd[l].reshape(1, ds),
                          g_ffn[l].reshape(1, d))
        gfin = g_final.reshape(1, d)
        cw, cb = conv_w[l].astype(F32), conv_b[l].reshape(1, dc).astype(F32)
        proj_a = dict(weights=[w_q, w_k, w_v, w_fd], out_map=[[0], [1, 3], [2, 4], [5]],
                      out_dtypes=[BF16, F32, F32, BF16, BF16, F32])
        proj_b = dict(weights=[w_z, w_x], out_map=[[0], [1]], out_dtypes=[F32, F32])

        tm = _pick_tile(mp, 512)
        q, k, v, kb, vb, fd = _norm_proj(yp, gm, tm=tm, name="in_proj_qkv", out_scale={0: dh ** -0.5 * LOG2E}, **proj_a)
        z, xbc = _norm_proj(yp, gm, tm=_pick_tile(mp, 256), name="in_proj_ssd", **proj_b)
        lf, dt, cs, cst = _gates(fd, bias_row, a_row, bp, seq, n_heads)
        oa = _attention(q, kb, vb, cst, ga, bp, seq, n_heads, dh, tq=_pick_tile(seq, 256))
        osm, st_p = _ssd_prompt(xbc, z, dt, cs, cst, cw, cb, dskip_row, gs, bp, seq, dims)
        x1, hf = _outproj(oa, osm, yp, w_oa, w_os, gf, tm, "out_proj")
        dff = wg.shape[1]
        tf = _pick_tile(dff, 512)
        yp_new = _ffn(hf, x1, wg, wu, wd, gfin, tm, tf, "ffn")

        qs, k_s, v_s, _, _, fds = _norm_proj(ys, gm, tm=bs, name="in_proj_qkv_s", out_scale={0: dh ** -0.5}, **proj_a)
        zs, xbcs = _norm_proj(ys, gm, tm=bs, name="in_proj_ssd_s", **proj_b)
        lfs, dts = _sgates(fds, bias_row)
        ck = cache_k[l].reshape(n_pool, page * n_heads, dh)
        cv = cache_v[l].reshape(n_pool, page * n_heads, dh)
        clf = cache_logf[l].astype(F32).reshape(n_pool, page * n_heads // LANES, LANES)
        n_pages = page_table.shape[1]
        oas = _decode_attention(page_table, qs.astype(F32).reshape(bs, n_heads, dh),
                                k_s.reshape(bs, n_heads, dh), v_s.reshape(bs, n_heads, dh),
                                lfs.reshape(bs, 1, LANES), ga.reshape(n_heads, dh), ck, cv, clf,
                                ppc=_pick_tile(n_pages, 16))
        oss, st_s, conv_s = _ssd_step(xbcs.reshape(bs, 1, dc), state_conv[l].astype(F32), zs.reshape(bs, 1, ds),
                                      dts.reshape(bs, 1, LANES), state_ssm[l].astype(F32).reshape(bs, nh * p, n_state),
                                      cw, cb, a_row, expand, dskip_row, gs, dims)
        x1s, hfs = _outproj(oas.reshape(bs, da), oss.reshape(bs, ds), ys, w_oa, w_os, gf, bs, "out_proj_s")
        ys_new = _ffn(hfs, x1s, wg, wu, wd, gfin, bs, tf, "ffn_s")

        yp, ys = yp_new, ys_new
        new = [k.reshape(bp, seq, n_heads, dh), v.reshape(bp, seq, n_heads, dh),
               lf[:, :n_heads].reshape(bp, seq, n_heads),
               k_s.reshape(bs, 1, n_heads, dh), v_s.reshape(bs, 1, n_heads, dh),
               lfs[:, :n_heads].reshape(bs, 1, n_heads),
               st_p.reshape(bp, nh, p, n_state), xbc.reshape(bp, seq, dc)[:, seq - kw + 1:],
               st_s.reshape(bs, nh, p, n_state), conv_s]
        for acc, val in zip(outs, new):
            acc.append(val)

    return (yp.reshape(bp, seq, d), ys.reshape(bs, 1, d)) + tuple(jnp.stack(o) for o in outs)
```

```python
import functools

import numpy as np
import jax
import jax.numpy as jnp
from jax import lax
from jax.experimental import pallas as pl
from jax.experimental.pallas import tpu as pltpu

F32 = jnp.float32
BF16 = jnp.bfloat16
EPS = 1e-6
LOG2E = 1.4426950408889634
LANES = 128
SUBLANES = 8
SSD_CHUNK = 128
VMEM_LIMIT_MB = 56
FFN_SUB = 256
DECODE_PAGES_PER_CHUNK = 8

_NT = (((1,), (1,)), ((), ()))
_TN = (((0,), (0,)), ((), ()))


def _params(sem):
    return pltpu.CompilerParams(dimension_semantics=sem, vmem_limit_bytes=VMEM_LIMIT_MB * 2**20)


def _resident(shape):
    nd = len(shape)
    return pl.BlockSpec(shape, lambda *_: (0,) * nd, pipeline_mode=pl.Buffered(1))


def _silu(x):
    return x / (1.0 + jnp.exp(-x))


def _split3(x):
    hi = x.astype(BF16)
    r1 = x - hi.astype(F32)
    mid = r1.astype(BF16)
    lo = (r1 - mid.astype(F32)).astype(BF16)
    return hi, mid, lo


def _dot_r01(a, b01):
    hi, mid, lo = _split3(a)
    d = functools.partial(jnp.dot, preferred_element_type=F32)
    return d(hi, b01) + d(mid, b01) + d(lo, b01)


def _dot_l01(a01, b):
    hi, mid, lo = _split3(b)
    d = functools.partial(jnp.dot, preferred_element_type=F32)
    return d(a01, hi) + d(a01, mid) + d(a01, lo)


def _norm_proj_body(x_ref, g_ref, *refs, n_w, out_map, out_scale, tn):
    w_refs, o_refs = refs[:n_w], refs[n_w:]
    x = x_ref[...]
    h = (x * lax.rsqrt(jnp.mean(x * x, axis=-1, keepdims=True) + EPS) * g_ref[...]).astype(BF16)
    for wi, w_ref in enumerate(w_refs):
        n = w_ref.shape[1]
        for n0 in range(0, n, tn):
            n1 = min(n0 + tn, n)
            r = jnp.dot(h, w_ref[:, n0:n1], preferred_element_type=F32)
            for oi in out_map[wi]:
                o_refs[oi][:, n0:n1] = (r * out_scale[oi] if oi in out_scale else r).astype(o_refs[oi].dtype)


def _norm_proj(x, g, weights, out_map, out_dtypes, tm, name, out_scale=None):
    m, d = x.shape
    widths = {}
    for wi, ois in enumerate(out_map):
        for oi in ois:
            widths[oi] = weights[wi].shape[1]
    n_out = len(out_dtypes)
    body = functools.partial(_norm_proj_body, n_w=len(weights), out_map=out_map, out_scale=out_scale or {}, tn=512)
    return pl.pallas_call(
        body,
        grid=(m // tm,),
        in_specs=[pl.BlockSpec((tm, d), lambda i: (i, 0)), _resident((1, d))]
        + [_resident(w.shape) for w in weights],
        out_specs=[pl.BlockSpec((tm, widths[oi]), lambda i: (i, 0)) for oi in range(n_out)],
        out_shape=[jax.ShapeDtypeStruct((m, widths[oi]), out_dtypes[oi]) for oi in range(n_out)],
        compiler_params=_params(("arbitrary",)),
        name=name,
    )(x, g, *weights)


def _gates_body(fd_ref, bias_ref, arow_ref, lf_ref, dt_ref, cs_ref, cst_ref, *, n_heads):
    t = fd_ref.shape[0]
    L = SSD_CHUNK
    lane = lax.broadcasted_iota(jnp.int32, (L, LANES), 1)
    tri = (lax.broadcasted_iota(jnp.int32, (L, L), 0) >= lax.broadcasted_iota(jnp.int32, (L, L), 1)).astype(BF16)
    is_f = lane < n_heads
    carry = jnp.zeros((1, LANES), F32)
    for c in range(t // L):
        rows = slice(c * L, (c + 1) * L)
        v = fd_ref[rows, :] + bias_ref[...]
        sp = jnp.log1p(jnp.exp(-jnp.abs(v)))
        lf = jnp.minimum(v, 0.0) - sp
        dt = jnp.maximum(v, 0.0) + sp
        zc = jnp.where(is_f, lf, dt * arow_ref[...])
        cs = _dot_l01(tri, zc) + jnp.where(is_f, carry, 0.0)
        carry = cs[L - 1:L, :]
        lf_ref[rows, :] = lf
        dt_ref[rows, :] = dt
        cs_ref[rows, :] = cs
        cst_ref[:, rows] = cs.T


def _gates(fd, bias_row, a_row, batch, seq, n_heads):
    m = fd.shape[0]
    row = lambda b: (b, 0)
    return pl.pallas_call(
        functools.partial(_gates_body, n_heads=n_heads),
        grid=(batch,),
        in_specs=[pl.BlockSpec((seq, LANES), row), _resident((1, LANES)), _resident((1, LANES))],
        out_specs=[pl.BlockSpec((seq, LANES), row)] * 3 + [pl.BlockSpec((None, LANES, seq), lambda b: (b, 0, 0))],
        out_shape=[jax.ShapeDtypeStruct((m, LANES), F32)] * 3 + [jax.ShapeDtypeStruct((batch, LANES, seq), F32)],
        compiler_params=_params(("arbitrary",)),
        name="gates",
    )(fd, bias_row, a_row)


def _sgates_body(fd_ref, bias_ref, lf_ref, dt_ref):
    v = fd_ref[...] + bias_ref[...]
    sp = jnp.log1p(jnp.exp(-jnp.abs(v)))
    lf_ref[...] = jnp.minimum(v, 0.0) - sp
    dt_ref[...] = jnp.maximum(v, 0.0) + sp


def _sgates(fd, bias_row):
    return pl.pallas_call(
        _sgates_body,
        out_shape=[jax.ShapeDtypeStruct(fd.shape, F32)] * 2,
        name="sample_gates",
    )(fd, bias_row)


def _attn_body(q_ref, k_ref, v_ref, ct_ref, g_ref, o_ref, m_ref, l_ref, acc_ref, *, tq, n_heads, dh):
    qi = pl.program_id(1)
    causal = lax.broadcasted_iota(jnp.int32, (tq, tq), 0) >= lax.broadcasted_iota(jnp.int32, (tq, tq), 1)
    m_ref[...] = jnp.full(m_ref.shape, -jnp.inf, F32)
    l_ref[...] = jnp.zeros(l_ref.shape, F32)
    acc_ref[...] = jnp.zeros(acc_ref.shape, F32)

    def kv_block(kj, masked):
        k0 = pl.multiple_of(kj * tq, tq)
        for h in range(n_heads):
            sl = slice(h * dh, (h + 1) * dh)
            s = lax.dot_general(q_ref[:, sl], k_ref[pl.ds(k0, tq), sl], _NT, preferred_element_type=F32)
            s = s - ct_ref[h:h + 1, pl.ds(k0, tq)] * LOG2E
            if masked:
                s = jnp.where(causal, s, -jnp.inf)
            m_old = m_ref[h]
            m_new = jnp.maximum(m_old, jnp.max(s, axis=-1, keepdims=True))
            alpha = jnp.exp2(m_old - m_new)
            p = jnp.exp2(s - jnp.concatenate([m_new] * (tq // LANES), axis=1))
            psum = p[:, :LANES]
            for c0 in range(LANES, tq, LANES):
                psum = psum + p[:, c0:c0 + LANES]
            l_ref[h] = alpha * l_ref[h] + psum
            acc_ref[:, sl] = alpha * acc_ref[:, sl] + jnp.dot(p.astype(BF16), v_ref[pl.ds(k0, tq), sl],
                                                              preferred_element_type=F32)
            m_ref[h] = m_new

    def loop_body(kj, carry):
        kv_block(kj, False)
        return carry

    lax.fori_loop(0, qi, loop_body, 0)
    kv_block(qi, True)
    for h in range(n_heads):
        sl = slice(h * dh, (h + 1) * dh)
        acc_ref[:, sl] = acc_ref[:, sl] / jnp.sum(l_ref[h], axis=-1, keepdims=True)
    o = acc_ref[...]
    o_ref[...] = (o * lax.rsqrt(jnp.mean(o * o, axis=-1, keepdims=True) + EPS) * g_ref[...]).astype(o_ref.dtype)


def _attention(q, kb, vb, cst, g_attn, batch, seq, n_heads, dh, tq):
    m, da = q.shape
    nq = seq // tq
    body = functools.partial(_attn_body, tq=tq, n_heads=n_heads, dh=dh)
    return pl.pallas_call(
        body,
        grid=(batch, nq),
        in_specs=[
            pl.BlockSpec((tq, da), lambda b, i: (b * nq + i, 0)),
            pl.BlockSpec((seq, da), lambda b, i: (b, 0)),
            pl.BlockSpec((seq, da), lambda b, i: (b, 0)),
            pl.BlockSpec((None, SUBLANES, seq), lambda b, i: (b, 0, 0)),
            _resident((1, da)),
        ],
        out_specs=pl.BlockSpec((tq, da), lambda b, i: (b * nq + i, 0)),
        out_shape=jax.ShapeDtypeStruct((m, da), BF16),
        scratch_shapes=[pltpu.VMEM((n_heads, tq, LANES), F32), pltpu.VMEM((n_heads, tq, LANES), F32),
                        pltpu.VMEM((tq, da), F32)],
        compiler_params=_params(("arbitrary", "arbitrary")),
        name="fox_attention",
    )(q, kb, vb, cst, g_attn)


def _ssd_body(xbc_ref, z_ref, dt_ref, cs_ref, cst_ref, cw_ref, cb_ref, dsk_ref, g_ref,
              o_ref, st_ref, xp_ref, act_ref, y_ref, *, ds, n_groups, n_state, n_heads, p, hoff, kw):
    L = SSD_CHUNK
    dc = xbc_ref.shape[1]
    c = pl.program_id(1)
    pad = SUBLANES

    @pl.when(c == 0)
    def _():
        st_ref[...] = jnp.zeros_like(st_ref)
        xp_ref[:, 0:pad, :] = jnp.zeros((dc // LANES, pad, LANES), F32)

    slab = 512
    for t in range(dc // LANES):
        sl = slice(t * LANES, (t + 1) * LANES)
        xp_ref[t, pad:pad + L, :] = xbc_ref[:, sl]
        a = cw_ref[0:1, sl] * xp_ref[t, pad - kw + 1:pad - kw + 1 + L, :]
        for i in range(1, kw):
            a = a + cw_ref[i:i + 1, sl] * xp_ref[t, pad - kw + 1 + i:pad - kw + 1 + i + L, :]
        a = cb_ref[:, sl] + a
        act_ref[:, sl] = _silu(a)
        xp_ref[t, 0:pad, :] = xp_ref[t, L:L + pad, :]

    dtv = dt_ref[...]
    cumv = cs_ref[...]
    cumt = cst_ref[...]
    lane = lax.broadcasted_iota(jnp.int32, (L, LANES), 1)
    causal = lax.broadcasted_iota(jnp.int32, (L, L), 0) >= lax.broadcasted_iota(jnp.int32, (L, L), 1)
    first = lane < p
    hpg = n_heads // n_groups
    gw = hpg * p
    for g in range(n_groups):
        bg = act_ref[:, ds + g * n_state:ds + (g + 1) * n_state].astype(BF16)
        cg = act_ref[:, ds + (n_groups + g) * n_state:ds + (n_groups + g + 1) * n_state].astype(BF16)
        cbm = lax.dot_general(cg, bg, _NT, preferred_element_type=F32)
        st_g = st_ref[g * gw:(g + 1) * gw, :]
        yoff = lax.dot_general(cg, st_g.astype(BF16), _NT, preferred_element_type=F32)
        xd_parts = []
        for pr in range(hpg // 2):
            j0 = g * hpg + 2 * pr
            la, lb = hoff + j0, hoff + j0 + 1
            c0v, c1v = cumv[:, la:la + 1], cumv[:, lb:lb + 1]
            d0 = jnp.exp(jnp.where(causal, c0v - cumt[la:la + 1, :], -jnp.inf))
            d1 = jnp.exp(jnp.where(causal, c1v - cumt[lb:lb + 1, :], -jnp.inf))
            mm = jnp.concatenate([(cbm * d0).astype(BF16), (cbm * d1).astype(BF16)], axis=1)
            cols = slice(j0 * p, (j0 + 2) * p)
            xdt = act_ref[:, cols] * jnp.where(first, dtv[:, la:la + 1], dtv[:, lb:lb + 1])
            bd = jnp.concatenate([jnp.where(first, xdt, 0.0).astype(BF16),
                                  jnp.where(first, 0.0, xdt).astype(BF16)], axis=0)
            ydiag = jnp.dot(mm, bd, preferred_element_type=F32)
            cum_pair = jnp.where(first, c0v, c1v)
            y_ref[:, cols] = ydiag + yoff[:, 2 * pr * p:(2 * pr + 2) * p] * jnp.exp(cum_pair)
            last_pair = jnp.where(first[0:1, :], cumv[L - 1:L, la:la + 1], cumv[L - 1:L, lb:lb + 1])
            xd_parts.append((xdt * jnp.exp(last_pair - cum_pair)).astype(BF16))
        xd = jnp.concatenate(xd_parts, axis=1)
        snew = lax.dot_general(xd, bg, _TN, preferred_element_type=F32)
        for jj in range(hpg):
            j = g * hpg + jj
            cd = jnp.exp(cumv[L - 1:L, hoff + j:hoff + j + 1])
            rows = slice(j * p, (j + 1) * p)
            st_ref[rows, :] = st_ref[rows, :] * cd + snew[jj * p:(jj + 1) * p, :]

    ssq = jnp.zeros((L, 1), F32)
    for c0 in range(0, ds, slab):
        sl = slice(c0, c0 + slab)
        y = (y_ref[:, sl] + dsk_ref[:, sl] * act_ref[:, sl]) * _silu(z_ref[:, sl])
        y_ref[:, sl] = y
        ssq = ssq + jnp.sum(y * y, axis=-1, keepdims=True)
    rinv = lax.rsqrt(ssq / ds + EPS)
    for c0 in range(0, ds, slab):
        sl = slice(c0, c0 + slab)
        o_ref[:, sl] = (y_ref[:, sl] * rinv * g_ref[:, sl]).astype(o_ref.dtype)


def _ssd_prompt(xbc, z, dt, cs, cst, conv_w, conv_b, dskip_row, g_ssd, batch, seq, dims):
    ds, n_groups, n_state, n_heads, p, hoff, kw = dims
    m, dc = xbc.shape
    L = SSD_CHUNK
    nc = seq // L
    row = lambda b, c: (b * nc + c, 0)
    body = functools.partial(_ssd_body, ds=ds, n_groups=n_groups, n_state=n_state, n_heads=n_heads,
                             p=p, hoff=hoff, kw=kw)
    return pl.pallas_call(
        body,
        grid=(batch, nc),
        in_specs=[
            pl.BlockSpec((L, dc), row),
            pl.BlockSpec((L, ds), row),
            pl.BlockSpec((L, LANES), row),
            pl.BlockSpec((L, LANES), row),
            pl.BlockSpec((None, LANES, L), lambda b, c: (b, 0, c)),
            _resident((kw, dc)), _resident((1, dc)), _resident((1, ds)), _resident((1, ds)),
        ],
        out_specs=[pl.BlockSpec((L, ds), row),
                   pl.BlockSpec((None, n_heads * p, n_state), lambda b, c: (b, 0, 0))],
        out_shape=[jax.ShapeDtypeStruct((m, ds), BF16),
                   jax.ShapeDtypeStruct((batch, n_heads * p, n_state), F32)],
        scratch_shapes=[pltpu.VMEM((dc // LANES, L + SUBLANES, LANES), F32), pltpu.VMEM((L, dc), F32),
                        pltpu.VMEM((L, ds), F32)],
        compiler_params=_params(("arbitrary", "arbitrary")),
        name="ssd_prompt",
    )(xbc, z, dt, cs, cst, conv_w, conv_b, dskip_row, g_ssd)


def _outproj_body(oa_ref, os_ref, x_ref, wa_ref, ws_ref, g_ref, x1_ref, hf_ref, *, tn):
    d = x_ref.shape[1]
    oa, osm = oa_ref[...], os_ref[...]
    for n0 in range(0, d, tn):
        sl = slice(n0, n0 + tn)
        mix = jnp.dot(oa, wa_ref[:, sl], preferred_element_type=F32) + jnp.dot(osm, ws_ref[:, sl], preferred_element_type=F32)
        x1_ref[:, sl] = x_ref[:, sl] + mix
    x1 = x1_ref[...]
    hf_ref[...] = (x1 * lax.rsqrt(jnp.mean(x1 * x1, axis=-1, keepdims=True) + EPS) * g_ref[...]).astype(hf_ref.dtype)


def _outproj(oa, osm, x, wa, ws, g_ffn, tm, name):
    m, d = x.shape
    row = lambda i: (i, 0)
    return pl.pallas_call(
        functools.partial(_outproj_body, tn=512),
        grid=(m // tm,),
        in_specs=[pl.BlockSpec((tm, oa.shape[1]), row), pl.BlockSpec((tm, osm.shape[1]), row),
                  pl.BlockSpec((tm, d), row), _resident(wa.shape), _resident(ws.shape), _resident((1, d))],
        out_specs=[pl.BlockSpec((tm, d), row)] * 2,
        out_shape=[jax.ShapeDtypeStruct((m, d), F32), jax.ShapeDtypeStruct((m, d), BF16)],
        compiler_params=_params(("arbitrary",)),
        name=name,
    )(oa, osm, x, wa, ws, g_ffn)


def _ffn_body(hf_ref, x1_ref, wg_ref, wu_ref, wd_ref, g_ref, o_ref):
    f = pl.program_id(1)

    @pl.when(f == 0)
    def _():
        o_ref[...] = x1_ref[...]

    hf = hf_ref[...]
    tf = wg_ref.shape[1]
    acts = []
    for c0 in range(0, tf, FFN_SUB):
        sl = slice(c0, min(c0 + FFN_SUB, tf))
        gate = jnp.dot(hf, wg_ref[:, sl], preferred_element_type=F32)
        up = jnp.dot(hf, wu_ref[:, sl], preferred_element_type=F32)
        acts.append((_silu(gate) * up).astype(BF16))
    act = acts[0] if len(acts) == 1 else jnp.concatenate(acts, axis=1)
    part = jnp.dot(act, wd_ref[...], preferred_element_type=F32)

    o_ref[...] += part

    @pl.when(f == pl.num_programs(1) - 1)
    def _():
        y = o_ref[...]
        o_ref[...] = y * lax.rsqrt(jnp.mean(y * y, axis=-1, keepdims=True) + EPS) * g_ref[...]


def _ffn(hf, x1, wg, wu, wd, g_final, tm, tf, name):
    m, d = x1.shape
    dff = wg.shape[1]
    return pl.pallas_call(
        _ffn_body,
        grid=(m // tm, dff // tf),
        in_specs=[pl.BlockSpec((tm, d), lambda i, f: (i, 0)), pl.BlockSpec((tm, d), lambda i, f: (i, 0)),
                  pl.BlockSpec((d, tf), lambda i, f: (0, f)), pl.BlockSpec((d, tf), lambda i, f: (0, f)),
                  pl.BlockSpec((tf, d), lambda i, f: (f, 0)), _resident((1, d))],
        out_specs=pl.BlockSpec((tm, d), lambda i, f: (i, 0)),
        out_shape=jax.ShapeDtypeStruct((m, d), F32),
        compiler_params=_params(("arbitrary", "arbitrary")),
        name=name,
    )(hf, x1, wg, wu, wd, g_final)


class _Decode:
    def __init__(self, pt_ref, in_refs, oa_ref, scratch, *, ppc, nch, total, n_heads, dh, rh):
        (self.q, self.kn, self.vn, self.lfn, self.g, self.uv, self.slt, self.ck, self.cv, self.clf) = in_refs
        (self.kbuf, self.vbuf, self.lfbuf, self.sem, self.m, self.l, self.acc, self.carry) = scratch
        self.pt, self.oa = pt_ref, oa_ref
        self.ppc, self.nch, self.total, self.n_heads, self.dh, self.rh = ppc, nch, total, n_heads, dh, rh
        self.lf_rows = rh // LANES

    def copies(self, t, slot):
        out = []
        for j in range(self.ppc):
            pid = self.pt[t * self.ppc + j]
            out.append(pltpu.make_async_copy(self.ck.at[pid], self.kbuf.at[slot, pl.ds(j * self.rh, self.rh), :],
                                             self.sem.at[0, slot]))
            out.append(pltpu.make_async_copy(self.cv.at[pid], self.vbuf.at[slot, pl.ds(j * self.rh, self.rh), :],
                                             self.sem.at[1, slot]))
            out.append(pltpu.make_async_copy(self.clf.at[pid],
                                             self.lfbuf.at[slot, pl.ds(j * self.lf_rows, self.lf_rows), :],
                                             self.sem.at[2, slot]))
        return out

    def start_first(self):
        for cp in self.copies(0, 0):
            cp.start()

    def _update(self, s, valid, v_rows):
        s = jnp.where(valid, s, -jnp.inf)
        m_old = self.m[...]
        m_new = jnp.maximum(m_old, jnp.max(s, axis=-1, keepdims=True))
        alpha = jnp.exp(m_old - m_new)
        pr = jnp.exp(s - m_new)
        self.l[...] = alpha * self.l[...] + jnp.sum(pr, axis=-1, keepdims=True)
        self.acc[...] = alpha * self.acc[...] + jnp.dot(pr, v_rows, preferred_element_type=F32)
        self.m[...] = m_new

    def chunk(self, t):
        n_heads, dh, nch = self.n_heads, self.dh, self.nch
        b = t // nch
        c = t - b * nch
        slot = lax.rem(t, 2)

        @pl.when(t + 1 < self.total)
        def _():
            for cp in self.copies(t + 1, 1 - slot):
                cp.start()

        for cp in self.copies(t, slot):
            cp.wait()

        @pl.when(c == 0)
        def _():
            self.m[...] = jnp.full(self.m.shape, -jnp.inf, F32)
            self.l[...] = jnp.zeros(self.l.shape, F32)
            self.acc[...] = jnp.zeros(self.acc.shape, F32)
            self.carry[...] = jnp.zeros(self.carry.shape, F32)

        q = self.q[b]
        lane = lax.broadcasted_iota(jnp.int32, (n_heads, LANES), 1)
        head_match = jnp.bitwise_and(lane, n_heads - 1) == lax.broadcasted_iota(jnp.int32, (n_heads, LANES), 0)

        nrow = self.ppc * self.lf_rows
        lfv = self.lfbuf[slot]
        r = jnp.dot(jnp.concatenate(_split3(lfv), axis=0), self.uv[...], preferred_element_type=F32)
        r = r[0:nrow] + r[nrow:2 * nrow] + r[2 * nrow:3 * nrow]
        within, rowtot = r[:, :LANES], r[:, LANES:]
        rr = jnp.dot(self.slt[...], jnp.concatenate(_split3(rowtot), axis=1), preferred_element_type=F32)
        offs = rr[:, :LANES] + rr[:, LANES:2 * LANES] + rr[:, 2 * LANES:]
        carry = self.carry[...]
        cs = within + offs + carry
        self.carry[...] = carry + offs[nrow - 1:nrow, :] + rowtot[nrow - 1:nrow, :]

        s = lax.dot_general(q, self.kbuf[slot], _NT, preferred_element_type=F32)
        bias = jnp.concatenate([jnp.broadcast_to(cs[i:i + 1, :], (n_heads, LANES)) for i in range(nrow)], axis=1)
        valid = jnp.concatenate([head_match] * nrow, axis=1)
        self._update(s - bias, valid, self.vbuf[slot])

        @pl.when(c == nch - 1)
        def _():
            zpad = jnp.zeros((LANES - n_heads, dh), F32)
            k_self = jnp.concatenate([zpad, self.kn[b]], axis=0)
            v_self = jnp.concatenate([zpad, self.vn[b]], axis=0)
            s_self = lax.dot_general(q, k_self, _NT, preferred_element_type=F32)
            c_t = self.carry[...] + self.lfn[b]
            self._update(s_self - c_t, jnp.logical_and(lane >= LANES - n_heads, head_match), v_self)
            o = self.acc[...] / self.l[...]
            ms = jnp.sum(jnp.sum(o * o, axis=-1, keepdims=True), axis=0, keepdims=True) / (n_heads * dh)
            self.oa[b] = o * lax.rsqrt(ms + EPS) * self.g[...]


def _decode_consts(nrow, n_heads):
    li = np.arange(LANES)
    same_head = (li[:, None] % n_heads) == (li[None, :] % n_heads)
    u = same_head & (li[:, None] <= li[None, :])
    ri = np.arange(nrow)
    slt = ri[None, :] < ri[:, None]
    as_bf16 = lambda a: jnp.asarray(a.astype(np.float32), BF16)
    return as_bf16(np.concatenate([u, same_head], axis=1)), as_bf16(slt)


def _ffn_decode_body(pt_ref, hf_ref, x1_ref, wg_ref, wu_ref, wd_ref, g_ref, *refs, cps, dcfg):
    dec_in, (o_ref, oa_ref), scratch = refs[:10], refs[10:12], refs[12:]
    act_ref, scratch = scratch[0], scratch[1:]
    dec = _Decode(pt_ref, dec_in, oa_ref, scratch, **dcfg)
    f = pl.program_id(1)
    nf = pl.num_programs(1)
    step = pl.program_id(0) * nf + f
    tf = wg_ref.shape[1]
    n_sub = tf // FFN_SUB

    @pl.when(step == 0)
    def _():
        dec.start_first()

    @pl.when(f == 0)
    def _():
        o_ref[...] = x1_ref[...]

    def gate_up(j):
        sl = slice(j * FFN_SUB, (j + 1) * FFN_SUB)
        hf = hf_ref[...]
        gate = jnp.dot(hf, wg_ref[:, sl], preferred_element_type=F32)
        up = jnp.dot(hf, wu_ref[:, sl], preferred_element_type=F32)
        act_ref[:, sl] = (_silu(gate) * up).astype(BF16)

    def down():
        d = o_ref.shape[1]
        for n0 in range(0, d, 512):
            o_ref[:, n0:n0 + 512] += jnp.dot(act_ref[...], wd_ref[:, n0:n0 + 512], preferred_element_type=F32)

    pieces = [functools.partial(gate_up, j) for j in range(n_sub)] + [down]
    per_piece = [cps // len(pieces) + (1 if i < cps % len(pieces) else 0) for i in range(len(pieces))]
    u = 0
    for piece, n_chunks in zip(pieces, per_piece):
        for _ in range(n_chunks):
            t = step * cps + u
            u += 1

            @pl.when(t < dec.total)
            def _(t=t):
                dec.chunk(t)

        piece()

    @pl.when(f == nf - 1)
    def _():
        y = o_ref[...]
        o_ref[...] = y * lax.rsqrt(jnp.mean(y * y, axis=-1, keepdims=True) + EPS) * g_ref[...]


def _ffn_decode(hf, x1, wg, wu, wd, g_final, tm, tf, page_table, q, k_new, v_new, lf_new, g_attn,
                cache_k, cache_v, cache_lf, ppc):
    m, d = x1.shape
    dff = wg.shape[1]
    nb, n_heads, dh = q.shape
    n_pages = page_table.shape[1]
    rh = cache_k.shape[1]
    nch = n_pages // ppc
    total = nb * nch
    lf_rows = rh // LANES
    uv, slt = _decode_consts(ppc * lf_rows, n_heads)
    n_steps = (m // tm) * (dff // tf)
    cps = -(-total // n_steps)
    dcfg = dict(ppc=ppc, nch=nch, total=total, n_heads=n_heads, dh=dh, rh=rh)
    whole = lambda a: pl.BlockSpec(a.shape, lambda i, f, pt: (0,) * a.ndim)
    any_spec = pl.BlockSpec(memory_space=pl.ANY)
    dec_args = (q, k_new, v_new, lf_new, g_attn, uv, slt)
    grid_spec = pltpu.PrefetchScalarGridSpec(
        num_scalar_prefetch=1,
        grid=(m // tm, dff // tf),
        in_specs=[pl.BlockSpec((tm, d), lambda i, f, pt: (i, 0)), pl.BlockSpec((tm, d), lambda i, f, pt: (i, 0)),
                  pl.BlockSpec((d, tf), lambda i, f, pt: (0, f)), pl.BlockSpec((d, tf), lambda i, f, pt: (0, f)),
                  pl.BlockSpec((tf, d), lambda i, f, pt: (f, 0)), whole(g_final)]
        + [whole(a) for a in dec_args] + [any_spec] * 3,
        out_specs=[pl.BlockSpec((tm, d), lambda i, f, pt: (i, 0)), pl.BlockSpec((nb, n_heads, dh), lambda i, f, pt: (0, 0, 0))],
        scratch_shapes=[
            pltpu.VMEM((tm, tf), BF16),
            pltpu.VMEM((2, ppc * rh, dh), F32), pltpu.VMEM((2, ppc * rh, dh), F32),
            pltpu.VMEM((2, ppc * lf_rows, LANES), F32),
            pltpu.SemaphoreType.DMA((3, 2)),
            pltpu.VMEM((n_heads, 1), F32), pltpu.VMEM((n_heads, 1), F32), pltpu.VMEM((n_heads, dh), F32),
            pltpu.VMEM((1, LANES), F32),
        ],
    )
    return pl.pallas_call(
        functools.partial(_ffn_decode_body, cps=cps, dcfg=dcfg),
        grid_spec=grid_spec,
        out_shape=[jax.ShapeDtypeStruct((m, d), F32), jax.ShapeDtypeStruct((nb, n_heads, dh), F32)],
        compiler_params=_params(("arbitrary", "arbitrary")),
        name="ffn_with_paged_decode",
    )(page_table.reshape(-1), hf, x1, wg, wu, wd, g_final, *dec_args, cache_k, cache_v, cache_lf)


def _ssd_step_body(xbc_ref, sc_ref, z_ref, dt_ref, st_ref, cw_ref, cb_ref, arow_ref, e_ref, dsk_ref, g_ref,
                   o_ref, sto_ref, sco_ref, *, ds, n_groups, n_state, n_heads, p, hoff, kw):
    xn = xbc_ref[...]
    a = cw_ref[0:1, :] * sc_ref[0:1, :]
    for i in range(1, kw - 1):
        a = a + cw_ref[i:i + 1, :] * sc_ref[i:i + 1, :]
    a = cb_ref[...] + (a + cw_ref[kw - 1:kw, :] * xn)
    act = _silu(a)
    for i in range(kw - 2):
        sco_ref[i:i + 1, :] = sc_ref[i + 1:i + 2, :]
    sco_ref[kw - 2:kw - 1, :] = xn

    dt = dt_ref[...]
    da = dt * arow_ref[...]
    dt_exp = _dot_r01(jnp.broadcast_to(dt, (SUBLANES, LANES)), e_ref[...])[0:1, :]
    decay = jnp.exp(da)
    xs = act[:, :ds]
    xdt = xs * dt_exp
    hpg = n_heads // n_groups
    gw = hpg * p
    eye = lax.broadcasted_iota(jnp.int32, (gw, gw), 0) == lax.broadcasted_iota(jnp.int32, (gw, gw), 1)
    y_parts = []
    for g in range(n_groups):
        bg = act[:, ds + g * n_state:ds + (g + 1) * n_state]
        cg = act[:, ds + (n_groups + g) * n_state:ds + (n_groups + g + 1) * n_state]
        diag = jnp.where(eye, jnp.broadcast_to(xdt[:, g * gw:(g + 1) * gw], (gw, gw)), 0.0).astype(BF16)
        outer = jnp.dot(diag, jnp.broadcast_to(bg, (gw, n_state)).astype(BF16), preferred_element_type=F32)
        for jj in range(hpg):
            j = g * hpg + jj
            rows = slice(j * p, (j + 1) * p)
            sto_ref[rows, :] = st_ref[rows, :] * decay[:, hoff + j:hoff + j + 1] + outer[jj * p:(jj + 1) * p, :]
        st_new = sto_ref[g * gw:(g + 1) * gw, :].astype(BF16)
        c8 = jnp.broadcast_to(cg, (SUBLANES, n_state)).astype(BF16)
        y_parts.append(lax.dot_general(c8, st_new, _NT, preferred_element_type=F32)[0:1, :])
    y = jnp.concatenate(y_parts, axis=1)
    y = (y + dsk_ref[...] * xs) * _silu(z_ref[...])
    o_ref[...] = (y * lax.rsqrt(jnp.mean(y * y, axis=-1, keepdims=True) + EPS) * g_ref[...]).astype(o_ref.dtype)


def _ssd_step(xbc, state_conv, z, dt, state, conv_w, conv_b, a_row, expand, dskip_row, g_ssd, dims):
    ds, n_groups, n_state, n_heads, p, hoff, kw = dims
    nb, _, dc = xbc.shape
    per_b = lambda b: (b, 0, 0)
    body = functools.partial(_ssd_step_body, ds=ds, n_groups=n_groups, n_state=n_state, n_heads=n_heads,
                             p=p, hoff=hoff, kw=kw)
    return pl.pallas_call(
        body,
        grid=(nb,),
        in_specs=[
            pl.BlockSpec((None, 1, dc), per_b), pl.BlockSpec((None, kw - 1, dc), per_b),
            pl.BlockSpec((None, 1, ds), per_b), pl.BlockSpec((None, 1, LANES), per_b),
            pl.BlockSpec((None, n_heads * p, n_state), per_b),
            _resident((kw, dc)), _resident((1, dc)), _resident((1, LANES)), _resident(expand.shape),
            _resident((1, ds)), _resident((1, ds)),
        ],
        out_specs=[pl.BlockSpec((None, 1, ds), per_b), pl.BlockSpec((None, n_heads * p, n_state), per_b),
                   pl.BlockSpec((None, kw - 1, dc), per_b)],
        out_shape=[jax.ShapeDtypeStruct((nb, 1, ds), BF16), jax.ShapeDtypeStruct(state.shape, F32),
                   jax.ShapeDtypeStruct(state_conv.shape, F32)],
        compiler_params=_params(("arbitrary",)),
        name="ssd_step",
    )(xbc, state_conv, z, dt, state, conv_w, conv_b, a_row, expand, dskip_row, g_ssd)


def _pick_tile(m, pref):
    t = min(m, pref)
    while m % t:
        t //= 2
    return t


def kernel(x_prompt, x_sample, cache_k, cache_v, cache_logf, state_ssm, state_conv, page_table, g_mix, w_in, b_fgate, conv_w, conv_b, dt_bias, a_log, d_skip, g_attn, g_ssd, w_out, g_ffn, w_gate, w_up, w_down, g_final):
    bp, seq, d = x_prompt.shape
    bs = x_sample.shape[0]
    depth, n_pool, page, n_heads, dh = cache_k.shape
    da = n_heads * dh
    nh, p, n_state = state_ssm.shape[2:]
    ds = nh * p
    kw = conv_w.shape[1]
    dc = conv_w.shape[2]
    n_groups = (dc - ds) // (2 * n_state)
    hoff = n_heads
    assert depth == 1, "the FFN kernel applies the final rmsnorm, so it must be the last layer"
    assert x_sample.shape[1] == 1 and seq % SSD_CHUNK == 0 and n_heads + nh <= LANES - n_heads
    assert n_heads & (n_heads - 1) == 0
    assert (page * n_heads) % LANES == 0 and p * 2 == LANES and (nh // n_groups) % 2 == 0
    dims = (ds, n_groups, n_state, nh, p, hoff, kw)
    o_f, o_z, o_x, o_dt = 3 * da, 3 * da + n_heads, 3 * da + n_heads + ds, 3 * da + n_heads + ds + dc
    mp = bp * seq

    expand_np = np.zeros((LANES, ds), np.float32)
    for j in range(nh):
        expand_np[hoff + j, j * p:(j + 1) * p] = 1.0
    expand = jnp.asarray(expand_np, BF16)

    yp = x_prompt.reshape(mp, d)
    ys = x_sample.reshape(bs, d)
    outs = [[] for _ in range(10)]
    for l in range(depth):
        w = w_in[l]
        w_q, w_k, w_v = (w[:, i * da:(i + 1) * da].astype(BF16) for i in range(3))
        w_f = w[:, o_f:o_f + n_heads]
        w_fd = jnp.concatenate([w_f, w[:, o_dt:o_dt + nh], jnp.zeros((d, LANES - 2 * n_heads - nh), w.dtype), w_f],
                               axis=1).astype(BF16)
        w_z = w[:, o_z:o_z + ds].astype(BF16)
        w_x = w[:, o_x:o_x + dc].astype(BF16)
        w_oa = w_out[l][:da].astype(BF16)
        w_os = w_out[l][da:].astype(BF16)
        wg, wu, wd = w_gate[l].astype(BF16), w_up[l].astype(BF16), w_down[l].astype(BF16)
        zpad = jnp.zeros((LANES - 2 * n_heads - nh,), F32)
        bias_row = jnp.concatenate([b_fgate[l], dt_bias[l], zpad, b_fgate[l]]).reshape(1, LANES)
        a_row = jnp.concatenate([jnp.zeros((n_heads,), F32), -jnp.exp(a_log[l].astype(F32)),
                                 jnp.zeros((LANES - n_heads - nh,), F32)]).reshape(1, LANES)
        dskip_row = jnp.repeat(d_skip[l].astype(F32), p).reshape(1, ds)
        gm, ga, gs, gf = (g_mix[l].reshape(1, d), g_attn[l].reshape(1, da), g_ssd[l].reshape(1, ds),
                          g_ffn[l].reshape(1, d))
        gfin = g_final.reshape(1, d)
        cw, cb = conv_w[l].astype(F32), conv_b[l].reshape(1, dc).astype(F32)
        proj_a = dict(weights=[w_q, w_k, w_v, w_fd], out_map=[[0], [1, 3], [2, 4], [5]],
                      out_dtypes=[BF16, F32, F32, BF16, BF16, F32])
        proj_b = dict(weights=[w_z, w_x], out_map=[[0], [1]], out_dtypes=[F32, F32])

        tm = _pick_tile(mp, 512)
        q, k, v, kb, vb, fd = _norm_proj(yp, gm, tm=tm, name="in_proj_qkv", out_scale={0: dh ** -0.5 * LOG2E}, **proj_a)
        z, xbc = _norm_proj(yp, gm, tm=_pick_tile(mp, 256), name="in_proj_ssd", **proj_b)
        lf, dt, cs, cst = _gates(fd, bias_row, a_row, bp, seq, n_heads)
        oa = _attention(q, kb, vb, cst, ga, bp, seq, n_heads, dh, tq=_pick_tile(seq, 256))
        osm, st_p = _ssd_prompt(xbc, z, dt, cs, cst, cw, cb, dskip_row, gs, bp, seq, dims)
        x1, hf = _outproj(oa, osm, yp, w_oa, w_os, gf, tm, "out_proj")

        qs, k_s, v_s, _, _, fds = _norm_proj(ys, gm, tm=bs, name="in_proj_qkv_s", out_scale={0: dh ** -0.5}, **proj_a)
        zs, xbcs = _norm_proj(ys, gm, tm=bs, name="in_proj_ssd_s", **proj_b)
        lfs, dts = _sgates(fds, bias_row)
        ck = cache_k[l].reshape(n_pool, page * n_heads, dh)
        cv = cache_v[l].reshape(n_pool, page * n_heads, dh)
        clf = cache_logf[l].astype(F32).reshape(n_pool, page * n_heads // LANES, LANES)
        dff = wg.shape[1]
        tf = _pick_tile(dff, 512)
        yp_new, oas = _ffn_decode(hf, x1, wg, wu, wd, gfin, tm, tf, page_table,
                                  qs.astype(F32).reshape(bs, n_heads, dh), k_s.reshape(bs, n_heads, dh),
                                  v_s.reshape(bs, n_heads, dh), lfs.reshape(bs, 1, LANES), ga.reshape(n_heads, dh),
                                  ck, cv, clf, ppc=_pick_tile(page_table.shape[1], DECODE_PAGES_PER_CHUNK))
        oss, st_s, conv_s = _ssd_step(xbcs.reshape(bs, 1, dc), state_conv[l].astype(F32), zs.reshape(bs, 1, ds),
                                      dts.reshape(bs, 1, LANES), state_ssm[l].astype(F32).reshape(bs, nh * p, n_state),
                                      cw, cb, a_row, expand, dskip_row, gs, dims)
        x1s, hfs = _outproj(oas.astype(BF16).reshape(bs, da), oss.reshape(bs, ds), ys, w_oa, w_os, gf, bs, "out_proj_s")
        ys_new = _ffn(hfs, x1s, wg, wu, wd, gfin, bs, tf, "ffn_s")

        yp, ys = yp_new, ys_new
        new = [k.reshape(bp, seq, n_heads, dh), v.reshape(bp, seq, n_heads, dh),
               lf[:, :n_heads].reshape(bp, seq, n_heads),
               k_s.reshape(bs, 1, n_heads, dh), v_s.reshape(bs, 1, n_heads, dh),
               lfs[:, :n_heads].reshape(bs, 1, n_heads),
               st_p.reshape(bp, nh, p, n_state), xbc.reshape(bp, seq, dc)[:, seq - kw + 1:],
               st_s.reshape(bs, nh, p, n_state), conv_s]
        for acc, val in zip(outs, new):
            acc.append(val)

    return (yp.reshape(bp, seq, d), ys.reshape(bs, 1, d)) + tuple(jnp.stack(o) for o in outs)
```

```python
import functools

import numpy as np
import jax
import jax.numpy as jnp
from jax import lax
from jax.experimental import pallas as pl
from jax.experimental.pallas import tpu as pltpu

F32 = jnp.float32
BF16 = jnp.bfloat16
EPS = 1e-6
LOG2E = 1.4426950408889634
LANES = 128
SUBLANES = 8
SSD_CHUNK = 128
VMEM_LIMIT_MB = 56
FFN_SUB = 256
DECODE_PAGES_PER_CHUNK = 8

_NT = (((1,), (1,)), ((), ()))
_TN = (((0,), (0,)), ((), ()))


def _params(sem):
    return pltpu.CompilerParams(dimension_semantics=sem, vmem_limit_bytes=VMEM_LIMIT_MB * 2**20)


def _resident(shape):
    nd = len(shape)
    return pl.BlockSpec(shape, lambda *_: (0,) * nd, pipeline_mode=pl.Buffered(1))


def _silu(x):
    return x / (1.0 + jnp.exp(-x))


def _split3(x):
    hi = x.astype(BF16)
    r1 = x - hi.astype(F32)
    mid = r1.astype(BF16)
    lo = (r1 - mid.astype(F32)).astype(BF16)
    return hi, mid, lo


def _dot_r01(a, b01):
    hi, mid, lo = _split3(a)
    d = functools.partial(jnp.dot, preferred_element_type=F32)
    return d(hi, b01) + d(mid, b01) + d(lo, b01)


def _dot_l01(a01, b):
    hi, mid, lo = _split3(b)
    d = functools.partial(jnp.dot, preferred_element_type=F32)
    return d(a01, hi) + d(a01, mid) + d(a01, lo)


def _norm_proj_body(x_ref, g_ref, *refs, n_w, out_map, out_scale, tn):
    w_refs, o_refs = refs[:n_w], refs[n_w:]
    x = x_ref[...]
    h = (x * lax.rsqrt(jnp.mean(x * x, axis=-1, keepdims=True) + EPS) * g_ref[...]).astype(BF16)
    for wi, w_ref in enumerate(w_refs):
        n = w_ref.shape[1]
        for n0 in range(0, n, tn):
            n1 = min(n0 + tn, n)
            r = jnp.dot(h, w_ref[:, n0:n1], preferred_element_type=F32)
            for oi in out_map[wi]:
                o_refs[oi][:, n0:n1] = (r * out_scale[oi] if oi in out_scale else r).astype(o_refs[oi].dtype)


def _norm_proj(x, g, weights, out_map, out_dtypes, tm, name, out_scale=None):
    m, d = x.shape
    widths = {}
    for wi, ois in enumerate(out_map):
        for oi in ois:
            widths[oi] = weights[wi].shape[1]
    n_out = len(out_dtypes)
    body = functools.partial(_norm_proj_body, n_w=len(weights), out_map=out_map, out_scale=out_scale or {}, tn=512)
    return pl.pallas_call(
        body,
        grid=(m // tm,),
        in_specs=[pl.BlockSpec((tm, d), lambda i: (i, 0)), _resident((1, d))]
        + [_resident(w.shape) for w in weights],
        out_specs=[pl.BlockSpec((tm, widths[oi]), lambda i: (i, 0)) for oi in range(n_out)],
        out_shape=[jax.ShapeDtypeStruct((m, widths[oi]), out_dtypes[oi]) for oi in range(n_out)],
        compiler_params=_params(("arbitrary",)),
        name=name,
    )(x, g, *weights)


def _gates_body(fd_ref, bias_ref, arow_ref, lf_ref, dt_ref, cs_ref, cst_ref, *, n_heads):
    t = fd_ref.shape[0]
    L = SSD_CHUNK
    lane = lax.broadcasted_iota(jnp.int32, (L, LANES), 1)
    tri = (lax.broadcasted_iota(jnp.int32, (L, L), 0) >= lax.broadcasted_iota(jnp.int32, (L, L), 1)).astype(BF16)
    is_f = lane < n_heads
    carry = jnp.zeros((1, LANES), F32)
    for c in range(t // L):
        rows = slice(c * L, (c + 1) * L)
        v = fd_ref[rows, :] + bias_ref[...]
        sp = jnp.log1p(jnp.exp(-jnp.abs(v)))
        lf = jnp.minimum(v, 0.0) - sp
        dt = jnp.maximum(v, 0.0) + sp
        zc = jnp.where(is_f, lf, dt * arow_ref[...])
        cs = _dot_l01(tri, zc) + jnp.where(is_f, carry, 0.0)
        carry = cs[L - 1:L, :]
        lf_ref[rows, :] = lf
        dt_ref[rows, :] = dt
        cs_ref[rows, :] = cs
        cst_ref[:, rows] = cs.T


def _gates(fd, bias_row, a_row, batch, seq, n_heads):
    m = fd.shape[0]
    row = lambda b: (b, 0)
    return pl.pallas_call(
        functools.partial(_gates_body, n_heads=n_heads),
        grid=(batch,),
        in_specs=[pl.BlockSpec((seq, LANES), row), _resident((1, LANES)), _resident((1, LANES))],
        out_specs=[pl.BlockSpec((seq, LANES), row)] * 3 + [pl.BlockSpec((None, LANES, seq), lambda b: (b, 0, 0))],
        out_shape=[jax.ShapeDtypeStruct((m, LANES), F32)] * 3 + [jax.ShapeDtypeStruct((batch, LANES, seq), F32)],
        compiler_params=_params(("arbitrary",)),
        name="gates",
    )(fd, bias_row, a_row)


def _sgates_body(fd_ref, bias_ref, lf_ref, dt_ref):
    v = fd_ref[...] + bias_ref[...]
    sp = jnp.log1p(jnp.exp(-jnp.abs(v)))
    lf_ref[...] = jnp.minimum(v, 0.0) - sp
    dt_ref[...] = jnp.maximum(v, 0.0) + sp


def _sgates(fd, bias_row):
    return pl.pallas_call(
        _sgates_body,
        out_shape=[jax.ShapeDtypeStruct(fd.shape, F32)] * 2,
        name="sample_gates",
    )(fd, bias_row)


def _attn_body(q_ref, k_ref, v_ref, ct_ref, g_ref, o_ref, m_ref, l_ref, acc_ref, *, tq, n_heads, dh):
    qi = pl.program_id(1)
    causal = lax.broadcasted_iota(jnp.int32, (tq, tq), 0) >= lax.broadcasted_iota(jnp.int32, (tq, tq), 1)
    m_ref[...] = jnp.full(m_ref.shape, -jnp.inf, F32)
    l_ref[...] = jnp.zeros(l_ref.shape, F32)
    acc_ref[...] = jnp.zeros(acc_ref.shape, F32)

    def kv_block(kj, masked):
        k0 = pl.multiple_of(kj * tq, tq)
        for h in range(n_heads):
            sl = slice(h * dh, (h + 1) * dh)
            s = lax.dot_general(q_ref[:, sl], k_ref[pl.ds(k0, tq), sl], _NT, preferred_element_type=F32)
            s = s - ct_ref[h:h + 1, pl.ds(k0, tq)] * LOG2E
            if masked:
                s = jnp.where(causal, s, -jnp.inf)
            m_old = m_ref[h]
            m_new = jnp.maximum(m_old, jnp.max(s, axis=-1, keepdims=True))
            alpha = jnp.exp2(m_old - m_new)
            p = jnp.exp2(s - jnp.concatenate([m_new] * (tq // LANES), axis=1))
            psum = p[:, :LANES]
            for c0 in range(LANES, tq, LANES):
                psum = psum + p[:, c0:c0 + LANES]
            l_ref[h] = alpha * l_ref[h] + psum
            acc_ref[:, sl] = alpha * acc_ref[:, sl] + jnp.dot(p.astype(BF16), v_ref[pl.ds(k0, tq), sl],
                                                              preferred_element_type=F32)
            m_ref[h] = m_new

    def loop_body(kj, carry):
        kv_block(kj, False)
        return carry

    lax.fori_loop(0, qi, loop_body, 0)
    kv_block(qi, True)
    for h in range(n_heads):
        sl = slice(h * dh, (h + 1) * dh)
        acc_ref[:, sl] = acc_ref[:, sl] / jnp.sum(l_ref[h], axis=-1, keepdims=True)
    o = acc_ref[...]
    o_ref[...] = (o * lax.rsqrt(jnp.mean(o * o, axis=-1, keepdims=True) + EPS) * g_ref[...]).astype(o_ref.dtype)


def _attention(q, kb, vb, cst, g_attn, batch, seq, n_heads, dh, tq):
    m, da = q.shape
    nq = seq // tq
    body = functools.partial(_attn_body, tq=tq, n_heads=n_heads, dh=dh)
    return pl.pallas_call(
        body,
        grid=(batch, nq),
        in_specs=[
            pl.BlockSpec((tq, da), lambda b, i: (b * nq + i, 0)),
            pl.BlockSpec((seq, da), lambda b, i: (b, 0)),
            pl.BlockSpec((seq, da), lambda b, i: (b, 0)),
            pl.BlockSpec((None, SUBLANES, seq), lambda b, i: (b, 0, 0)),
            _resident((1, da)),
        ],
        out_specs=pl.BlockSpec((tq, da), lambda b, i: (b * nq + i, 0)),
        out_shape=jax.ShapeDtypeStruct((m, da), BF16),
        scratch_shapes=[pltpu.VMEM((n_heads, tq, LANES), F32), pltpu.VMEM((n_heads, tq, LANES), F32),
                        pltpu.VMEM((tq, da), F32)],
        compiler_params=_params(("arbitrary", "arbitrary")),
        name="fox_attention",
    )(q, kb, vb, cst, g_attn)


def _ssd_body(xbc_ref, z_ref, dt_ref, cs_ref, cst_ref, cw_ref, cb_ref, dsk_ref, g_ref,
              o_ref, st_ref, xp_ref, act_ref, y_ref, *, ds, n_groups, n_state, n_heads, p, hoff, kw):
    L = SSD_CHUNK
    dc = xbc_ref.shape[1]
    c = pl.program_id(1)
    pad = SUBLANES

    @pl.when(c == 0)
    def _():
        st_ref[...] = jnp.zeros_like(st_ref)
        xp_ref[:, 0:pad, :] = jnp.zeros((dc // LANES, pad, LANES), F32)

    slab = 512
    for t in range(dc // LANES):
        sl = slice(t * LANES, (t + 1) * LANES)
        xp_ref[t, pad:pad + L, :] = xbc_ref[:, sl]
        a = cw_ref[0:1, sl] * xp_ref[t, pad - kw + 1:pad - kw + 1 + L, :]
        for i in range(1, kw):
            a = a + cw_ref[i:i + 1, sl] * xp_ref[t, pad - kw + 1 + i:pad - kw + 1 + i + L, :]
        a = cb_ref[:, sl] + a
        act_ref[:, sl] = _silu(a)
        xp_ref[t, 0:pad, :] = xp_ref[t, L:L + pad, :]

    dtv = dt_ref[...]
    cumv = cs_ref[...]
    cumt = cst_ref[...]
    lane = lax.broadcasted_iota(jnp.int32, (L, LANES), 1)
    causal = lax.broadcasted_iota(jnp.int32, (L, L), 0) >= lax.broadcasted_iota(jnp.int32, (L, L), 1)
    first = lane < p
    hpg = n_heads // n_groups
    gw = hpg * p
    for g in range(n_groups):
        bg = act_ref[:, ds + g * n_state:ds + (g + 1) * n_state].astype(BF16)
        cg = act_ref[:, ds + (n_groups + g) * n_state:ds + (n_groups + g + 1) * n_state].astype(BF16)
        cbm = lax.dot_general(cg, bg, _NT, preferred_element_type=F32)
        st_g = st_ref[g * gw:(g + 1) * gw, :]
        yoff = lax.dot_general(cg, st_g.astype(BF16), _NT, preferred_element_type=F32)
        xd_parts = []
        for pr in range(hpg // 2):
            j0 = g * hpg + 2 * pr
            la, lb = hoff + j0, hoff + j0 + 1
            c0v, c1v = cumv[:, la:la + 1], cumv[:, lb:lb + 1]
            d0 = jnp.exp(jnp.where(causal, c0v - cumt[la:la + 1, :], -jnp.inf))
            d1 = jnp.exp(jnp.where(causal, c1v - cumt[lb:lb + 1, :], -jnp.inf))
            mm = jnp.concatenate([(cbm * d0).astype(BF16), (cbm * d1).astype(BF16)], axis=1)
            cols = slice(j0 * p, (j0 + 2) * p)
            xdt = act_ref[:, cols] * jnp.where(first, dtv[:, la:la + 1], dtv[:, lb:lb + 1])
            bd = jnp.concatenate([jnp.where(first, xdt, 0.0).astype(BF16),
                                  jnp.where(first, 0.0, xdt).astype(BF16)], axis=0)
            ydiag = jnp.dot(mm, bd, preferred_element_type=F32)
            cum_pair = jnp.where(first, c0v, c1v)
            y_ref[:, cols] = ydiag + yoff[:, 2 * pr * p:(2 * pr + 2) * p] * jnp.exp(cum_pair)
            last_pair = jnp.where(first[0:1, :], cumv[L - 1:L, la:la + 1], cumv[L - 1:L, lb:lb + 1])
            xd_parts.append((xdt * jnp.exp(last_pair - cum_pair)).astype(BF16))
        xd = jnp.concatenate(xd_parts, axis=1)
        snew = lax.dot_general(xd, bg, _TN, preferred_element_type=F32)
        for jj in range(hpg):
            j = g * hpg + jj
            cd = jnp.exp(cumv[L - 1:L, hoff + j:hoff + j + 1])
            rows = slice(j * p, (j + 1) * p)
            st_ref[rows, :] = st_ref[rows, :] * cd + snew[jj * p:(jj + 1) * p, :]

    ssq = jnp.zeros((L, 1), F32)
    for c0 in range(0, ds, slab):
        sl = slice(c0, c0 + slab)
        y = (y_ref[:, sl] + dsk_ref[:, sl] * act_ref[:, sl]) * _silu(z_ref[:, sl])
        y_ref[:, sl] = y
        ssq = ssq + jnp.sum(y * y, axis=-1, keepdims=True)
    rinv = lax.rsqrt(ssq / ds + EPS)
    for c0 in range(0, ds, slab):
        sl = slice(c0, c0 + slab)
        o_ref[:, sl] = (y_ref[:, sl] * rinv * g_ref[:, sl]).astype(o_ref.dtype)


def _ssd_prompt(xbc, z, dt, cs, cst, conv_w, conv_b, dskip_row, g_ssd, batch, seq, dims):
    ds, n_groups, n_state, n_heads, p, hoff, kw = dims
    m, dc = xbc.shape
    L = SSD_CHUNK
    nc = seq // L
    row = lambda b, c: (b * nc + c, 0)
    body = functools.partial(_ssd_body, ds=ds, n_groups=n_groups, n_state=n_state, n_heads=n_heads,
                             p=p, hoff=hoff, kw=kw)
    return pl.pallas_call(
        body,
        grid=(batch, nc),
        in_specs=[
            pl.BlockSpec((L, dc), row),
            pl.BlockSpec((L, ds), row),
            pl.BlockSpec((L, LANES), row),
            pl.BlockSpec((L, LANES), row),
            pl.BlockSpec((None, LANES, L), lambda b, c: (b, 0, c)),
            _resident((kw, dc)), _resident((1, dc)), _resident((1, ds)), _resident((1, ds)),
        ],
        out_specs=[pl.BlockSpec((L, ds), row),
                   pl.BlockSpec((None, n_heads * p, n_state), lambda b, c: (b, 0, 0))],
        out_shape=[jax.ShapeDtypeStruct((m, ds), BF16),
                   jax.ShapeDtypeStruct((batch, n_heads * p, n_state), F32)],
        scratch_shapes=[pltpu.VMEM((dc // LANES, L + SUBLANES, LANES), F32), pltpu.VMEM((L, dc), F32),
                        pltpu.VMEM((L, ds), F32)],
        compiler_params=_params(("arbitrary", "arbitrary")),
        name="ssd_prompt",
    )(xbc, z, dt, cs, cst, conv_w, conv_b, dskip_row, g_ssd)


def _outproj_body(oa_ref, os_ref, x_ref, wa_ref, ws_ref, g_ref, x1_ref, hf_ref, *, tn):
    d = x_ref.shape[1]
    oa, osm = oa_ref[...], os_ref[...]
    for n0 in range(0, d, tn):
        sl = slice(n0, n0 + tn)
        mix = jnp.dot(oa, wa_ref[:, sl], preferred_element_type=F32) + jnp.dot(osm, ws_ref[:, sl], preferred_element_type=F32)
        x1_ref[:, sl] = x_ref[:, sl] + mix
    x1 = x1_ref[...]
    hf_ref[...] = (x1 * lax.rsqrt(jnp.mean(x1 * x1, axis=-1, keepdims=True) + EPS) * g_ref[...]).astype(hf_ref.dtype)


def _outproj(oa, osm, x, wa, ws, g_ffn, tm, name):
    m, d = x.shape
    row = lambda i: (i, 0)
    return pl.pallas_call(
        functools.partial(_outproj_body, tn=512),
        grid=(m // tm,),
        in_specs=[pl.BlockSpec((tm, oa.shape[1]), row), pl.BlockSpec((tm, osm.shape[1]), row),
                  pl.BlockSpec((tm, d), row), _resident(wa.shape), _resident(ws.shape), _resident((1, d))],
        out_specs=[pl.BlockSpec((tm, d), row)] * 2,
        out_shape=[jax.ShapeDtypeStruct((m, d), F32), jax.ShapeDtypeStruct((m, d), BF16)],
        compiler_params=_params(("arbitrary",)),
        name=name,
    )(oa, osm, x, wa, ws, g_ffn)


def _ffn_body(hf_ref, x1_ref, wg_ref, wu_ref, wd_ref, g_ref, o_ref):
    f = pl.program_id(1)

    @pl.when(f == 0)
    def _():
        o_ref[...] = x1_ref[...]

    hf = hf_ref[...]
    tf = wg_ref.shape[1]
    acts = []
    for c0 in range(0, tf, FFN_SUB):
        sl = slice(c0, min(c0 + FFN_SUB, tf))
        gate = jnp.dot(hf, wg_ref[:, sl], preferred_element_type=F32)
        up = jnp.dot(hf, wu_ref[:, sl], preferred_element_type=F32)
        acts.append((_silu(gate) * up).astype(BF16))
    act = acts[0] if len(acts) == 1 else jnp.concatenate(acts, axis=1)
    part = jnp.dot(act, wd_ref[...], preferred_element_type=F32)

    o_ref[...] += part

    @pl.when(f == pl.num_programs(1) - 1)
    def _():
        y = o_ref[...]
        o_ref[...] = y * lax.rsqrt(jnp.mean(y * y, axis=-1, keepdims=True) + EPS) * g_ref[...]


def _ffn(hf, x1, wg, wu, wd, g_final, tm, tf, name):
    m, d = x1.shape
    dff = wg.shape[1]
    return pl.pallas_call(
        _ffn_body,
        grid=(m // tm, dff // tf),
        in_specs=[pl.BlockSpec((tm, d), lambda i, f: (i, 0)), pl.BlockSpec((tm, d), lambda i, f: (i, 0)),
                  pl.BlockSpec((d, tf), lambda i, f: (0, f)), pl.BlockSpec((d, tf), lambda i, f: (0, f)),
                  pl.BlockSpec((tf, d), lambda i, f: (f, 0)), _resident((1, d))],
        out_specs=pl.BlockSpec((tm, d), lambda i, f: (i, 0)),
        out_shape=jax.ShapeDtypeStruct((m, d), F32),
        compiler_params=_params(("arbitrary", "arbitrary")),
        name=name,
    )(hf, x1, wg, wu, wd, g_final)


class _Decode:
    def __init__(self, pt_ref, in_refs, oa_ref, scratch, *, ppc, nch, total, n_heads, dh, rh):
        (self.q, self.kn, self.vn, self.lfn, self.g, self.uv, self.slt, self.ck, self.cv, self.clf) = in_refs
        (self.kbuf, self.vbuf, self.lfbuf, self.sem, self.m, self.l, self.acc, self.carry) = scratch
        self.pt, self.oa = pt_ref, oa_ref
        self.ppc, self.nch, self.total, self.n_heads, self.dh, self.rh = ppc, nch, total, n_heads, dh, rh
        self.lf_rows = rh // LANES

    def key_copies(self, t, slot):
        out = []
        for j in range(self.ppc):
            pid = self.pt[t * self.ppc + j]
            out.append(pltpu.make_async_copy(self.ck.at[pid], self.kbuf.at[slot, pl.ds(j * self.rh, self.rh), :],
                                             self.sem.at[0, slot]))
            out.append(pltpu.make_async_copy(self.clf.at[pid],
                                             self.lfbuf.at[slot, pl.ds(j * self.lf_rows, self.lf_rows), :],
                                             self.sem.at[2, slot]))
        return out

    def value_copies(self, t, slot):
        out = []
        for j in range(self.ppc):
            pid = self.pt[t * self.ppc + j]
            out.append(pltpu.make_async_copy(self.cv.at[pid], self.vbuf.at[slot, pl.ds(j * self.rh, self.rh), :],
                                             self.sem.at[1, slot]))
        return out

    def start_first(self):
        for t in range(min(2, self.total)):
            for cp in self.key_copies(t, t) + self.value_copies(t, t):
                cp.start()

    def _update(self, s, valid, v_rows):
        s = jnp.where(valid, s, -jnp.inf)
        m_old = self.m[...]
        m_new = jnp.maximum(m_old, jnp.max(s, axis=-1, keepdims=True))
        alpha = jnp.exp(m_old - m_new)
        pr = jnp.exp(s - m_new)
        self.l[...] = alpha * self.l[...] + jnp.sum(pr, axis=-1, keepdims=True)
        self.acc[...] = alpha * self.acc[...] + jnp.dot(pr, v_rows, preferred_element_type=F32)
        self.m[...] = m_new

    def chunk(self, t):
        n_heads, dh, nch = self.n_heads, self.dh, self.nch
        b = t // nch
        c = t - b * nch
        slot = lax.rem(t, 2)

        for cp in self.key_copies(t, slot):
            cp.wait()

        @pl.when(c == 0)
        def _():
            self.m[...] = jnp.full(self.m.shape, -jnp.inf, F32)
            self.l[...] = jnp.zeros(self.l.shape, F32)
            self.acc[...] = jnp.zeros(self.acc.shape, F32)
            self.carry[...] = jnp.zeros(self.carry.shape, F32)

        q = self.q[b]
        lane = lax.broadcasted_iota(jnp.int32, (n_heads, LANES), 1)
        head_match = jnp.bitwise_and(lane, n_heads - 1) == lax.broadcasted_iota(jnp.int32, (n_heads, LANES), 0)

        nrow = self.ppc * self.lf_rows
        lfv = self.lfbuf[slot]
        r = jnp.dot(jnp.concatenate(_split3(lfv), axis=0), self.uv[...], preferred_element_type=F32)
        r = r[0:nrow] + r[nrow:2 * nrow] + r[2 * nrow:3 * nrow]
        within, rowtot = r[:, :LANES], r[:, LANES:]
        rr = jnp.dot(self.slt[...], jnp.concatenate(_split3(rowtot), axis=1), preferred_element_type=F32)
        offs = rr[:, :LANES] + rr[:, LANES:2 * LANES] + rr[:, 2 * LANES:]
        carry = self.carry[...]
        cs = within + offs + carry
        self.carry[...] = carry + offs[nrow - 1:nrow, :] + rowtot[nrow - 1:nrow, :]

        s = lax.dot_general(q, self.kbuf[slot], _NT, preferred_element_type=F32)
        bias = jnp.concatenate([jnp.broadcast_to(cs[i:i + 1, :], (n_heads, LANES)) for i in range(nrow)], axis=1)
        valid = jnp.concatenate([head_match] * nrow, axis=1)
        sb = s - bias

        @pl.when(t + 2 < self.total)
        def _():
            for cp in self.key_copies(t + 2, slot):
                cp.start()

        for cp in self.value_copies(t, slot):
            cp.wait()
        self._update(sb, valid, self.vbuf[slot])

        @pl.when(t + 2 < self.total)
        def _():
            for cp in self.value_copies(t + 2, slot):
                cp.start()

        @pl.when(c == nch - 1)
        def _():
            zpad = jnp.zeros((LANES - n_heads, dh), F32)
            k_self = jnp.concatenate([zpad, self.kn[b]], axis=0)
            v_self = jnp.concatenate([zpad, self.vn[b]], axis=0)
            s_self = lax.dot_general(q, k_self, _NT, preferred_element_type=F32)
            c_t = self.carry[...] + self.lfn[b]
            self._update(s_self - c_t, jnp.logical_and(lane >= LANES - n_heads, head_match), v_self)
            o = self.acc[...] / self.l[...]
            ms = jnp.sum(jnp.sum(o * o, axis=-1, keepdims=True), axis=0, keepdims=True) / (n_heads * dh)
            self.oa[b] = o * lax.rsqrt(ms + EPS) * self.g[...]


def _decode_consts(nrow, n_heads):
    li = np.arange(LANES)
    same_head = (li[:, None] % n_heads) == (li[None, :] % n_heads)
    u = same_head & (li[:, None] <= li[None, :])
    ri = np.arange(nrow)
    slt = ri[None, :] < ri[:, None]
    as_bf16 = lambda a: jnp.asarray(a.astype(np.float32), BF16)
    return as_bf16(np.concatenate([u, same_head], axis=1)), as_bf16(slt)


def _ffn_decode_body(pt_ref, hf_ref, x1_ref, wg_ref, wu_ref, wd_ref, g_ref, *refs, cps, dcfg):
    dec_in, (o_ref, oa_ref), scratch = refs[:10], refs[10:12], refs[12:]
    act_ref, scratch = scratch[0], scratch[1:]
    dec = _Decode(pt_ref, dec_in, oa_ref, scratch, **dcfg)
    f = pl.program_id(1)
    nf = pl.num_programs(1)
    step = pl.program_id(0) * nf + f
    tf = wg_ref.shape[1]
    n_sub = tf // FFN_SUB

    @pl.when(step == 0)
    def _():
        dec.start_first()

    @pl.when(f == 0)
    def _():
        o_ref[...] = x1_ref[...]

    def gate_up(j):
        sl = slice(j * FFN_SUB, (j + 1) * FFN_SUB)
        hf = hf_ref[...]
        gate = jnp.dot(hf, wg_ref[:, sl], preferred_element_type=F32)
        up = jnp.dot(hf, wu_ref[:, sl], preferred_element_type=F32)
        act_ref[:, sl] = (_silu(gate) * up).astype(BF16)

    def down():
        d = o_ref.shape[1]
        for n0 in range(0, d, 512):
            o_ref[:, n0:n0 + 512] += jnp.dot(act_ref[...], wd_ref[:, n0:n0 + 512], preferred_element_type=F32)

    pieces = [functools.partial(gate_up, j) for j in range(n_sub)] + [down]
    per_piece = [cps // len(pieces) + (1 if i < cps % len(pieces) else 0) for i in range(len(pieces))]
    u = 0
    for piece, n_chunks in zip(pieces, per_piece):
        for _ in range(n_chunks):
            t = step * cps + u
            u += 1

            @pl.when(t < dec.total)
            def _(t=t):
                dec.chunk(t)

        piece()

    @pl.when(f == nf - 1)
    def _():
        y = o_ref[...]
        o_ref[...] = y * lax.rsqrt(jnp.mean(y * y, axis=-1, keepdims=True) + EPS) * g_ref[...]


def _ffn_decode(hf, x1, wg, wu, wd, g_final, tm, tf, page_table, q, k_new, v_new, lf_new, g_attn,
                cache_k, cache_v, cache_lf, ppc):
    m, d = x1.shape
    dff = wg.shape[1]
    nb, n_heads, dh = q.shape
    n_pages = page_table.shape[1]
    rh = cache_k.shape[1]
    nch = n_pages // ppc
    total = nb * nch
    lf_rows = rh // LANES
    uv, slt = _decode_consts(ppc * lf_rows, n_heads)
    n_steps = (m // tm) * (dff // tf)
    cps = -(-total // n_steps)
    dcfg = dict(ppc=ppc, nch=nch, total=total, n_heads=n_heads, dh=dh, rh=rh)
    whole = lambda a: pl.BlockSpec(a.shape, lambda i, f, pt: (0,) * a.ndim)
    any_spec = pl.BlockSpec(memory_space=pl.ANY)
    dec_args = (q, k_new, v_new, lf_new, g_attn, uv, slt)
    grid_spec = pltpu.PrefetchScalarGridSpec(
        num_scalar_prefetch=1,
        grid=(m // tm, dff // tf),
        in_specs=[pl.BlockSpec((tm, d), lambda i, f, pt: (i, 0), pipeline_mode=pl.Buffered(1)),
                  pl.BlockSpec((tm, d), lambda i, f, pt: (i, 0), pipeline_mode=pl.Buffered(1)),
                  pl.BlockSpec((d, tf), lambda i, f, pt: (0, f)), pl.BlockSpec((d, tf), lambda i, f, pt: (0, f)),
                  pl.BlockSpec((tf, d), lambda i, f, pt: (f, 0)), whole(g_final)]
        + [whole(a) for a in dec_args] + [any_spec] * 3,
        out_specs=[pl.BlockSpec((tm, d), lambda i, f, pt: (i, 0), pipeline_mode=pl.Buffered(1)),
                   pl.BlockSpec((nb, n_heads, dh), lambda i, f, pt: (0, 0, 0))],
        scratch_shapes=[
            pltpu.VMEM((tm, tf), BF16),
            pltpu.VMEM((2, ppc * rh, dh), F32), pltpu.VMEM((2, ppc * rh, dh), F32),
            pltpu.VMEM((2, ppc * lf_rows, LANES), F32),
            pltpu.SemaphoreType.DMA((3, 2)),
            pltpu.VMEM((n_heads, 1), F32), pltpu.VMEM((n_heads, 1), F32), pltpu.VMEM((n_heads, dh), F32),
            pltpu.VMEM((1, LANES), F32),
        ],
    )
    return pl.pallas_call(
        functools.partial(_ffn_decode_body, cps=cps, dcfg=dcfg),
        grid_spec=grid_spec,
        out_shape=[jax.ShapeDtypeStruct((m, d), F32), jax.ShapeDtypeStruct((nb, n_heads, dh), F32)],
        compiler_params=_params(("arbitrary", "arbitrary")),
        name="ffn_with_paged_decode",
    )(page_table.reshape(-1), hf, x1, wg, wu, wd, g_final, *dec_args, cache_k, cache_v, cache_lf)


def _ssd_step_body(xbc_ref, sc_ref, z_ref, dt_ref, st_ref, cw_ref, cb_ref, arow_ref, e_ref, dsk_ref, g_ref,
                   o_ref, sto_ref, sco_ref, *, ds, n_groups, n_state, n_heads, p, hoff, kw):
    xn = xbc_ref[...]
    a = cw_ref[0:1, :] * sc_ref[0:1, :]
    for i in range(1, kw - 1):
        a = a + cw_ref[i:i + 1, :] * sc_ref[i:i + 1, :]
    a = cb_ref[...] + (a + cw_ref[kw - 1:kw, :] * xn)
    act = _silu(a)
    for i in range(kw - 2):
        sco_ref[i:i + 1, :] = sc_ref[i + 1:i + 2, :]
    sco_ref[kw - 2:kw - 1, :] = xn

    dt = dt_ref[...]
    da = dt * arow_ref[...]
    dt_exp = _dot_r01(jnp.broadcast_to(dt, (SUBLANES, LANES)), e_ref[...])[0:1, :]
    decay = jnp.exp(da)
    xs = act[:, :ds]
    xdt = xs * dt_exp
    hpg = n_heads // n_groups
    gw = hpg * p
    eye = lax.broadcasted_iota(jnp.int32, (gw, gw), 0) == lax.broadcasted_iota(jnp.int32, (gw, gw), 1)
    y_parts = []
    for g in range(n_groups):
        bg = act[:, ds + g * n_state:ds + (g + 1) * n_state]
        cg = act[:, ds + (n_groups + g) * n_state:ds + (n_groups + g + 1) * n_state]
        diag = jnp.where(eye, jnp.broadcast_to(xdt[:, g * gw:(g + 1) * gw], (gw, gw)), 0.0).astype(BF16)
        outer = jnp.dot(diag, jnp.broadcast_to(bg, (gw, n_state)).astype(BF16), preferred_element_type=F32)
        for jj in range(hpg):
            j = g * hpg + jj
            rows = slice(j * p, (j + 1) * p)
            sto_ref[rows, :] = st_ref[rows, :] * decay[:, hoff + j:hoff + j + 1] + outer[jj * p:(jj + 1) * p, :]
        st_new = sto_ref[g * gw:(g + 1) * gw, :].astype(BF16)
        c8 = jnp.broadcast_to(cg, (SUBLANES, n_state)).astype(BF16)
        y_parts.append(lax.dot_general(c8, st_new, _NT, preferred_element_type=F32)[0:1, :])
    y = jnp.concatenate(y_parts, axis=1)
    y = (y + dsk_ref[...] * xs) * _silu(z_ref[...])
    o_ref[...] = (y * lax.rsqrt(jnp.mean(y * y, axis=-1, keepdims=True) + EPS) * g_ref[...]).astype(o_ref.dtype)


def _ssd_step(xbc, state_conv, z, dt, state, conv_w, conv_b, a_row, expand, dskip_row, g_ssd, dims):
    ds, n_groups, n_state, n_heads, p, hoff, kw = dims
    nb, _, dc = xbc.shape
    per_b = lambda b: (b, 0, 0)
    body = functools.partial(_ssd_step_body, ds=ds, n_groups=n_groups, n_state=n_state, n_heads=n_heads,
                             p=p, hoff=hoff, kw=kw)
    return pl.pallas_call(
        body,
        grid=(nb,),
        in_specs=[
            pl.BlockSpec((None, 1, dc), per_b), pl.BlockSpec((None, kw - 1, dc), per_b),
            pl.BlockSpec((None, 1, ds), per_b), pl.BlockSpec((None, 1, LANES), per_b),
            pl.BlockSpec((None, n_heads * p, n_state), per_b),
            _resident((kw, dc)), _resident((1, dc)), _resident((1, LANES)), _resident(expand.shape),
            _resident((1, ds)), _resident((1, ds)),
        ],
        out_specs=[pl.BlockSpec((None, 1, ds), per_b), pl.BlockSpec((None, n_heads * p, n_state), per_b),
                   pl.BlockSpec((None, kw - 1, dc), per_b)],
        out_shape=[jax.ShapeDtypeStruct((nb, 1, ds), BF16), jax.ShapeDtypeStruct(state.shape, F32),
                   jax.ShapeDtypeStruct(state_conv.shape, F32)],
        compiler_params=_params(("arbitrary",)),
        name="ssd_step",
    )(xbc, state_conv, z, dt, state, conv_w, conv_b, a_row, expand, dskip_row, g_ssd)


def _pick_tile(m, pref):
    t = min(m, pref)
    while m % t:
        t //= 2
    return t


def kernel(x_prompt, x_sample, cache_k, cache_v, cache_logf, state_ssm, state_conv, page_table, g_mix, w_in, b_fgate, conv_w, conv_b, dt_bias, a_log, d_skip, g_attn, g_ssd, w_out, g_ffn, w_gate, w_up, w_down, g_final):
    bp, seq, d = x_prompt.shape
    bs = x_sample.shape[0]
    depth, n_pool, page, n_heads, dh = cache_k.shape
    da = n_heads * dh
    nh, p, n_state = state_ssm.shape[2:]
    ds = nh * p
    kw = conv_w.shape[1]
    dc = conv_w.shape[2]
    n_groups = (dc - ds) // (2 * n_state)
    hoff = n_heads
    assert depth == 1, "the FFN kernel applies the final rmsnorm, so it must be the last layer"
    assert x_sample.shape[1] == 1 and seq % SSD_CHUNK == 0 and n_heads + nh <= LANES - n_heads
    assert n_heads & (n_heads - 1) == 0
    assert (page * n_heads) % LANES == 0 and p * 2 == LANES and (nh // n_groups) % 2 == 0
    dims = (ds, n_groups, n_state, nh, p, hoff, kw)
    o_f, o_z, o_x, o_dt = 3 * da, 3 * da + n_heads, 3 * da + n_heads + ds, 3 * da + n_heads + ds + dc
    mp = bp * seq

    expand_np = np.zeros((LANES, ds), np.float32)
    for j in range(nh):
        expand_np[hoff + j, j * p:(j + 1) * p] = 1.0
    expand = jnp.asarray(expand_np, BF16)

    yp = x_prompt.reshape(mp, d)
    ys = x_sample.reshape(bs, d)
    outs = [[] for _ in range(10)]
    for l in range(depth):
        w = w_in[l]
        w_q, w_k, w_v = (w[:, i * da:(i + 1) * da].astype(BF16) for i in range(3))
        w_f = w[:, o_f:o_f + n_heads]
        w_fd = jnp.concatenate([w_f, w[:, o_dt:o_dt + nh], jnp.zeros((d, LANES - 2 * n_heads - nh), w.dtype), w_f],
                               axis=1).astype(BF16)
        w_z = w[:, o_z:o_z + ds].astype(BF16)
        w_x = w[:, o_x:o_x + dc].astype(BF16)
        w_oa = w_out[l][:da].astype(BF16)
        w_os = w_out[l][da:].astype(BF16)
        wg, wu, wd = w_gate[l].astype(BF16), w_up[l].astype(BF16), w_down[l].astype(BF16)
        zpad = jnp.zeros((LANES - 2 * n_heads - nh,), F32)
        bias_row = jnp.concatenate([b_fgate[l], dt_bias[l], zpad, b_fgate[l]]).reshape(1, LANES)
        a_row = jnp.concatenate([jnp.zeros((n_heads,), F32), -jnp.exp(a_log[l].astype(F32)),
                                 jnp.zeros((LANES - n_heads - nh,), F32)]).reshape(1, LANES)
        dskip_row = jnp.repeat(d_skip[l].astype(F32), p).reshape(1, ds)
        gm, ga, gs, gf = (g_mix[l].reshape(1, d), g_attn[l].reshape(1, da), g_ssd[l].reshape(1, ds),
                          g_ffn[l].reshape(1, d))
        gfin = g_final.reshape(1, d)
        cw, cb = conv_w[l].astype(F32), conv_b[l].reshape(1, dc).astype(F32)
        proj_a = dict(weights=[w_q, w_k, w_v, w_fd], out_map=[[0], [1, 3], [2, 4], [5]],
                      out_dtypes=[BF16, F32, F32, BF16, BF16, F32])
        proj_b = dict(weights=[w_z, w_x], out_map=[[0], [1]], out_dtypes=[F32, F32])

        tm = _pick_tile(mp, 512)
        q, k, v, kb, vb, fd = _norm_proj(yp, gm, tm=tm, name="in_proj_qkv", out_scale={0: dh ** -0.5 * LOG2E}, **proj_a)
        z, xbc = _norm_proj(yp, gm, tm=_pick_tile(mp, 256), name="in_proj_ssd", **proj_b)
        lf, dt, cs, cst = _gates(fd, bias_row, a_row, bp, seq, n_heads)
        oa = _attention(q, kb, vb, cst, ga, bp, seq, n_heads, dh, tq=_pick_tile(seq, 256))
        osm, st_p = _ssd_prompt(xbc, z, dt, cs, cst, cw, cb, dskip_row, gs, bp, seq, dims)
        x1, hf = _outproj(oa, osm, yp, w_oa, w_os, gf, tm, "out_proj")

        qs, k_s, v_s, _, _, fds = _norm_proj(ys, gm, tm=bs, name="in_proj_qkv_s", out_scale={0: dh ** -0.5}, **proj_a)
        zs, xbcs = _norm_proj(ys, gm, tm=bs, name="in_proj_ssd_s", **proj_b)
        lfs, dts = _sgates(fds, bias_row)
        ck = cache_k[l].reshape(n_pool, page * n_heads, dh)
        cv = cache_v[l].reshape(n_pool, page * n_heads, dh)
        clf = cache_logf[l].astype(F32).reshape(n_pool, page * n_heads // LANES, LANES)
        dff = wg.shape[1]
        tf = _pick_tile(dff, 512)
        yp_new, oas = _ffn_decode(hf, x1, wg, wu, wd, gfin, _pick_tile(mp, 1024), tf, page_table,
                                  qs.astype(F32).reshape(bs, n_heads, dh), k_s.reshape(bs, n_heads, dh),
                                  v_s.reshape(bs, n_heads, dh), lfs.reshape(bs, 1, LANES), ga.reshape(n_heads, dh),
                                  ck, cv, clf, ppc=_pick_tile(page_table.shape[1], DECODE_PAGES_PER_CHUNK))
        oss, st_s, conv_s = _ssd_step(xbcs.reshape(bs, 1, dc), state_conv[l].astype(F32), zs.reshape(bs, 1, ds),
                                      dts.reshape(bs, 1, LANES), state_ssm[l].astype(F32).reshape(bs, nh * p, n_state),
                                      cw, cb, a_row, expand, dskip_row, gs, dims)
        x1s, hfs = _outproj(oas.astype(BF16).reshape(bs, da), oss.reshape(bs, ds), ys, w_oa, w_os, gf, bs, "out_proj_s")
        ys_new = _ffn(hfs, x1s, wg, wu, wd, gfin, bs, tf, "ffn_s")

        yp, ys = yp_new, ys_new
        new = [k.reshape(bp, seq, n_heads, dh), v.reshape(bp, seq, n_heads, dh),
               lf[:, :n_heads].reshape(bp, seq, n_heads),
               k_s.reshape(bs, 1, n_heads, dh), v_s.reshape(bs, 1, n_heads, dh),
               lfs[:, :n_heads].reshape(bs, 1, n_heads),
               st_p.reshape(bp, nh, p, n_state), xbc.reshape(bp, seq, dc)[:, seq - kw + 1:],
               st_s.reshape(bs, nh, p, n_state), conv_s]
        for acc, val in zip(outs, new):
            acc.append(val)

    return (yp.reshape(bp, seq, d), ys.reshape(bs, 1, d)) + tuple(jnp.stack(o) for o in outs)
```

```python
import functools

import numpy as np
import jax
import jax.numpy as jnp
from jax import lax
from jax.experimental import pallas as pl
from jax.experimental.pallas import tpu as pltpu

F32 = jnp.float32
BF16 = jnp.bfloat16
EPS = 1e-6
LOG2E = 1.4426950408889634
LANES = 128
SUBLANES = 8
SSD_CHUNK = 128
VMEM_LIMIT_MB = 56
FFN_SUB = 256
DECODE_PAGES_PER_CHUNK = 8

_NT = (((1,), (1,)), ((), ()))
_TN = (((0,), (0,)), ((), ()))


def _params(sem):
    return pltpu.CompilerParams(dimension_semantics=sem, vmem_limit_bytes=VMEM_LIMIT_MB * 2**20)


def _resident(shape):
    nd = len(shape)
    return pl.BlockSpec(shape, lambda *_: (0,) * nd, pipeline_mode=pl.Buffered(1))


def _silu(x):
    return x / (1.0 + jnp.exp(-x))


def _split3(x):
    hi = x.astype(BF16)
    r1 = x - hi.astype(F32)
    mid = r1.astype(BF16)
    lo = (r1 - mid.astype(F32)).astype(BF16)
    return hi, mid, lo


def _dot_r01(a, b01):
    hi, mid, lo = _split3(a)
    d = functools.partial(jnp.dot, preferred_element_type=F32)
    return d(hi, b01) + d(mid, b01) + d(lo, b01)


def _dot_l01(a01, b):
    hi, mid, lo = _split3(b)
    d = functools.partial(jnp.dot, preferred_element_type=F32)
    return d(a01, hi) + d(a01, mid) + d(a01, lo)


def _norm_proj_body(x_ref, g_ref, *refs, n_w, out_map, out_scale, tn):
    w_refs, o_refs = refs[:n_w], refs[n_w:]
    x = x_ref[...]
    h = (x * lax.rsqrt(jnp.mean(x * x, axis=-1, keepdims=True) + EPS) * g_ref[...]).astype(BF16)
    for wi, w_ref in enumerate(w_refs):
        n = w_ref.shape[1]
        for n0 in range(0, n, tn):
            n1 = min(n0 + tn, n)
            r = jnp.dot(h, w_ref[:, n0:n1], preferred_element_type=F32)
            for oi in out_map[wi]:
                o_refs[oi][:, n0:n1] = (r * out_scale[oi] if oi in out_scale else r).astype(o_refs[oi].dtype)


def _norm_proj(x, g, weights, out_map, out_dtypes, tm, name, out_scale=None):
    m, d = x.shape
    widths = {}
    for wi, ois in enumerate(out_map):
        for oi in ois:
            widths[oi] = weights[wi].shape[1]
    n_out = len(out_dtypes)
    body = functools.partial(_norm_proj_body, n_w=len(weights), out_map=out_map, out_scale=out_scale or {}, tn=512)
    return pl.pallas_call(
        body,
        grid=(m // tm,),
        in_specs=[pl.BlockSpec((tm, d), lambda i: (i, 0)), _resident((1, d))]
        + [_resident(w.shape) for w in weights],
        out_specs=[pl.BlockSpec((tm, widths[oi]), lambda i: (i, 0)) for oi in range(n_out)],
        out_shape=[jax.ShapeDtypeStruct((m, widths[oi]), out_dtypes[oi]) for oi in range(n_out)],
        compiler_params=_params(("arbitrary",)),
        name=name,
    )(x, g, *weights)


def _gates_body(fd_ref, bias_ref, arow_ref, lf_ref, dt_ref, cs_ref, cst_ref, *, n_heads):
    t = fd_ref.shape[0]
    L = SSD_CHUNK
    lane = lax.broadcasted_iota(jnp.int32, (L, LANES), 1)
    tri = (lax.broadcasted_iota(jnp.int32, (L, L), 0) >= lax.broadcasted_iota(jnp.int32, (L, L), 1)).astype(BF16)
    is_f = lane < n_heads
    carry = jnp.zeros((1, LANES), F32)
    for c in range(t // L):
        rows = slice(c * L, (c + 1) * L)
        v = fd_ref[rows, :] + bias_ref[...]
        sp = jnp.log1p(jnp.exp(-jnp.abs(v)))
        lf = jnp.minimum(v, 0.0) - sp
        dt = jnp.maximum(v, 0.0) + sp
        zc = jnp.where(is_f, lf, dt * arow_ref[...])
        cs = _dot_l01(tri, zc) + jnp.where(is_f, carry, 0.0)
        carry = cs[L - 1:L, :]
        lf_ref[rows, :] = lf
        dt_ref[rows, :] = dt
        cs_ref[rows, :] = cs
        cst_ref[:, rows] = cs.T


def _gates(fd, bias_row, a_row, batch, seq, n_heads):
    m = fd.shape[0]
    row = lambda b: (b, 0)
    return pl.pallas_call(
        functools.partial(_gates_body, n_heads=n_heads),
        grid=(batch,),
        in_specs=[pl.BlockSpec((seq, LANES), row), _resident((1, LANES)), _resident((1, LANES))],
        out_specs=[pl.BlockSpec((seq, LANES), row)] * 3 + [pl.BlockSpec((None, LANES, seq), lambda b: (b, 0, 0))],
        out_shape=[jax.ShapeDtypeStruct((m, LANES), F32)] * 3 + [jax.ShapeDtypeStruct((batch, LANES, seq), F32)],
        compiler_params=_params(("arbitrary",)),
        name="gates",
    )(fd, bias_row, a_row)


def _sgates_body(fd_ref, bias_ref, lf_ref, dt_ref):
    v = fd_ref[...] + bias_ref[...]
    sp = jnp.log1p(jnp.exp(-jnp.abs(v)))
    lf_ref[...] = jnp.minimum(v, 0.0) - sp
    dt_ref[...] = jnp.maximum(v, 0.0) + sp


def _sgates(fd, bias_row):
    return pl.pallas_call(
        _sgates_body,
        out_shape=[jax.ShapeDtypeStruct(fd.shape, F32)] * 2,
        name="sample_gates",
    )(fd, bias_row)


def _attn_body(q_ref, k_ref, v_ref, ct_ref, g_ref, o_ref, m_ref, l_ref, acc_ref, *, tq, n_heads, dh):
    qi = pl.program_id(1)
    causal = lax.broadcasted_iota(jnp.int32, (tq, tq), 0) >= lax.broadcasted_iota(jnp.int32, (tq, tq), 1)
    m_ref[...] = jnp.full(m_ref.shape, -jnp.inf, F32)
    l_ref[...] = jnp.zeros(l_ref.shape, F32)
    acc_ref[...] = jnp.zeros(acc_ref.shape, F32)

    def kv_block(kj, masked):
        k0 = pl.multiple_of(kj * tq, tq)
        for h in range(n_heads):
            sl = slice(h * dh, (h + 1) * dh)
            s = lax.dot_general(q_ref[:, sl], k_ref[pl.ds(k0, tq), sl], _NT, preferred_element_type=F32)
            s = s - ct_ref[h:h + 1, pl.ds(k0, tq)] * LOG2E
            if masked:
                s = jnp.where(causal, s, -jnp.inf)
            m_old = m_ref[h]
            m_new = jnp.maximum(m_old, jnp.max(s, axis=-1, keepdims=True))
            alpha = jnp.exp2(m_old - m_new)
            p = jnp.exp2(s - jnp.concatenate([m_new] * (tq // LANES), axis=1))
            psum = p[:, :LANES]
            for c0 in range(LANES, tq, LANES):
                psum = psum + p[:, c0:c0 + LANES]
            l_ref[h] = alpha * l_ref[h] + psum
            acc_ref[:, sl] = alpha * acc_ref[:, sl] + jnp.dot(p.astype(BF16), v_ref[pl.ds(k0, tq), sl],
                                                              preferred_element_type=F32)
            m_ref[h] = m_new

    def loop_body(kj, carry):
        kv_block(kj, False)
        return carry

    lax.fori_loop(0, qi, loop_body, 0)
    kv_block(qi, True)
    for h in range(n_heads):
        sl = slice(h * dh, (h + 1) * dh)
        acc_ref[:, sl] = acc_ref[:, sl] / jnp.sum(l_ref[h], axis=-1, keepdims=True)
    o = acc_ref[...]
    o_ref[...] = (o * lax.rsqrt(jnp.mean(o * o, axis=-1, keepdims=True) + EPS) * g_ref[...]).astype(o_ref.dtype)


def _attention(q, kb, vb, cst, g_attn, batch, seq, n_heads, dh, tq):
    m, da = q.shape
    nq = seq // tq
    body = functools.partial(_attn_body, tq=tq, n_heads=n_heads, dh=dh)
    return pl.pallas_call(
        body,
        grid=(batch, nq),
        in_specs=[
            pl.BlockSpec((tq, da), lambda b, i: (b * nq + i, 0)),
            pl.BlockSpec((seq, da), lambda b, i: (b, 0)),
            pl.BlockSpec((seq, da), lambda b, i: (b, 0)),
            pl.BlockSpec((None, SUBLANES, seq), lambda b, i: (b, 0, 0)),
            _resident((1, da)),
        ],
        out_specs=pl.BlockSpec((tq, da), lambda b, i: (b * nq + i, 0)),
        out_shape=jax.ShapeDtypeStruct((m, da), BF16),
        scratch_shapes=[pltpu.VMEM((n_heads, tq, LANES), F32), pltpu.VMEM((n_heads, tq, LANES), F32),
                        pltpu.VMEM((tq, da), F32)],
        compiler_params=_params(("arbitrary", "arbitrary")),
        name="fox_attention",
    )(q, kb, vb, cst, g_attn)


def _ssd_body(xbc_ref, z_ref, dt_ref, cs_ref, cst_ref, cw_ref, cb_ref, dsk_ref, g_ref,
              o_ref, st_ref, xp_ref, act_ref, y_ref, *, ds, n_groups, n_state, n_heads, p, hoff, kw):
    L = SSD_CHUNK
    dc = xbc_ref.shape[1]
    c = pl.program_id(1)
    pad = SUBLANES

    @pl.when(c == 0)
    def _():
        st_ref[...] = jnp.zeros_like(st_ref)
        xp_ref[:, 0:pad, :] = jnp.zeros((dc // LANES, pad, LANES), F32)

    slab = 512
    for t in range(dc // LANES):
        sl = slice(t * LANES, (t + 1) * LANES)
        xp_ref[t, pad:pad + L, :] = xbc_ref[:, sl]
        a = cw_ref[0:1, sl] * xp_ref[t, pad - kw + 1:pad - kw + 1 + L, :]
        for i in range(1, kw):
            a = a + cw_ref[i:i + 1, sl] * xp_ref[t, pad - kw + 1 + i:pad - kw + 1 + i + L, :]
        a = cb_ref[:, sl] + a
        act_ref[:, sl] = _silu(a)
        xp_ref[t, 0:pad, :] = xp_ref[t, L:L + pad, :]

    dtv = dt_ref[...]
    cumv = cs_ref[...]
    cumt = cst_ref[...]
    lane = lax.broadcasted_iota(jnp.int32, (L, LANES), 1)
    causal = lax.broadcasted_iota(jnp.int32, (L, L), 0) >= lax.broadcasted_iota(jnp.int32, (L, L), 1)
    first = lane < p
    hpg = n_heads // n_groups
    gw = hpg * p
    for g in range(n_groups):
        bg = act_ref[:, ds + g * n_state:ds + (g + 1) * n_state].astype(BF16)
        cg = act_ref[:, ds + (n_groups + g) * n_state:ds + (n_groups + g + 1) * n_state].astype(BF16)
        cbm = lax.dot_general(cg, bg, _NT, preferred_element_type=F32)
        st_g = st_ref[g * gw:(g + 1) * gw, :]
        yoff = lax.dot_general(cg, st_g.astype(BF16), _NT, preferred_element_type=F32)
        xd_parts = []
        for pr in range(hpg // 2):
            j0 = g * hpg + 2 * pr
            la, lb = hoff + j0, hoff + j0 + 1
            c0v, c1v = cumv[:, la:la + 1], cumv[:, lb:lb + 1]
            d0 = jnp.exp(jnp.where(causal, c0v - cumt[la:la + 1, :], -jnp.inf))
            d1 = jnp.exp(jnp.where(causal, c1v - cumt[lb:lb + 1, :], -jnp.inf))
            mm = jnp.concatenate([(cbm * d0).astype(BF16), (cbm * d1).astype(BF16)], axis=1)
            cols = slice(j0 * p, (j0 + 2) * p)
            xdt = act_ref[:, cols] * jnp.where(first, dtv[:, la:la + 1], dtv[:, lb:lb + 1])
            bd = jnp.concatenate([jnp.where(first, xdt, 0.0).astype(BF16),
                                  jnp.where(first, 0.0, xdt).astype(BF16)], axis=0)
            ydiag = jnp.dot(mm, bd, preferred_element_type=F32)
            cum_pair = jnp.where(first, c0v, c1v)
            y_ref[:, cols] = ydiag + yoff[:, 2 * pr * p:(2 * pr + 2) * p] * jnp.exp(cum_pair)
            last_pair = jnp.where(first[0:1, :], cumv[L - 1:L, la:la + 1], cumv[L - 1:L, lb:lb + 1])
            xd_parts.append((xdt * jnp.exp(last_pair - cum_pair)).astype(BF16))
        xd = jnp.concatenate(xd_parts, axis=1)
        snew = lax.dot_general(xd, bg, _TN, preferred_element_type=F32)
        for jj in range(hpg):
            j = g * hpg + jj
            cd = jnp.exp(cumv[L - 1:L, hoff + j:hoff + j + 1])
            rows = slice(j * p, (j + 1) * p)
            st_ref[rows, :] = st_ref[rows, :] * cd + snew[jj * p:(jj + 1) * p, :]

    ssq = jnp.zeros((L, 1), F32)
    for c0 in range(0, ds, slab):
        sl = slice(c0, c0 + slab)
        y = (y_ref[:, sl] + dsk_ref[:, sl] * act_ref[:, sl]) * _silu(z_ref[:, sl])
        y_ref[:, sl] = y
        ssq = ssq + jnp.sum(y * y, axis=-1, keepdims=True)
    rinv = lax.rsqrt(ssq / ds + EPS)
    for c0 in range(0, ds, slab):
        sl = slice(c0, c0 + slab)
        o_ref[:, sl] = (y_ref[:, sl] * rinv * g_ref[:, sl]).astype(o_ref.dtype)


def _ssd_prompt(xbc, z, dt, cs, cst, conv_w, conv_b, dskip_row, g_ssd, batch, seq, dims):
    ds, n_groups, n_state, n_heads, p, hoff, kw = dims
    m, dc = xbc.shape
    L = SSD_CHUNK
    nc = seq // L
    row = lambda b, c: (b * nc + c, 0)
    body = functools.partial(_ssd_body, ds=ds, n_groups=n_groups, n_state=n_state, n_heads=n_heads,
                             p=p, hoff=hoff, kw=kw)
    return pl.pallas_call(
        body,
        grid=(batch, nc),
        in_specs=[
            pl.BlockSpec((L, dc), row),
            pl.BlockSpec((L, ds), row),
            pl.BlockSpec((L, LANES), row),
            pl.BlockSpec((L, LANES), row),
            pl.BlockSpec((None, LANES, L), lambda b, c: (b, 0, c)),
            _resident((kw, dc)), _resident((1, dc)), _resident((1, ds)), _resident((1, ds)),
        ],
        out_specs=[pl.BlockSpec((L, ds), row),
                   pl.BlockSpec((None, n_heads * p, n_state), lambda b, c: (b, 0, 0))],
        out_shape=[jax.ShapeDtypeStruct((m, ds), BF16),
                   jax.ShapeDtypeStruct((batch, n_heads * p, n_state), F32)],
        scratch_shapes=[pltpu.VMEM((dc // LANES, L + SUBLANES, LANES), F32), pltpu.VMEM((L, dc), F32),
                        pltpu.VMEM((L, ds), F32)],
        compiler_params=_params(("arbitrary", "arbitrary")),
        name="ssd_prompt",
    )(xbc, z, dt, cs, cst, conv_w, conv_b, dskip_row, g_ssd)


def _outproj_body(oa_ref, os_ref, x_ref, wa_ref, ws_ref, g_ref, x1_ref, hf_ref, *, tn):
    d = x_ref.shape[1]
    oa, osm = oa_ref[...], os_ref[...]
    for n0 in range(0, d, tn):
        sl = slice(n0, n0 + tn)
        mix = jnp.dot(oa, wa_ref[:, sl], preferred_element_type=F32) + jnp.dot(osm, ws_ref[:, sl], preferred_element_type=F32)
        x1_ref[:, sl] = x_ref[:, sl] + mix
    x1 = x1_ref[...]
    hf_ref[...] = (x1 * lax.rsqrt(jnp.mean(x1 * x1, axis=-1, keepdims=True) + EPS) * g_ref[...]).astype(hf_ref.dtype)


def _outproj(oa, osm, x, wa, ws, g_ffn, tm, name):
    m, d = x.shape
    row = lambda i: (i, 0)
    return pl.pallas_call(
        functools.partial(_outproj_body, tn=512),
        grid=(m // tm,),
        in_specs=[pl.BlockSpec((tm, oa.shape[1]), row), pl.BlockSpec((tm, osm.shape[1]), row),
                  pl.BlockSpec((tm, d), row), _resident(wa.shape), _resident(ws.shape), _resident((1, d))],
        out_specs=[pl.BlockSpec((tm, d), row)] * 2,
        out_shape=[jax.ShapeDtypeStruct((m, d), F32), jax.ShapeDtypeStruct((m, d), BF16)],
        compiler_params=_params(("arbitrary",)),
        name=name,
    )(oa, osm, x, wa, ws, g_ffn)


def _ffn_body(hf_ref, x1_ref, wg_ref, wu_ref, wd_ref, g_ref, o_ref):
    f = pl.program_id(1)

    @pl.when(f == 0)
    def _():
        o_ref[...] = x1_ref[...]

    hf = hf_ref[...]
    tf = wg_ref.shape[1]
    acts = []
    for c0 in range(0, tf, FFN_SUB):
        sl = slice(c0, min(c0 + FFN_SUB, tf))
        gate = jnp.dot(hf, wg_ref[:, sl], preferred_element_type=F32)
        up = jnp.dot(hf, wu_ref[:, sl], preferred_element_type=F32)
        acts.append((_silu(gate) * up).astype(BF16))
    act = acts[0] if len(acts) == 1 else jnp.concatenate(acts, axis=1)
    part = jnp.dot(act, wd_ref[...], preferred_element_type=F32)

    o_ref[...] += part

    @pl.when(f == pl.num_programs(1) - 1)
    def _():
        y = o_ref[...]
        o_ref[...] = y * lax.rsqrt(jnp.mean(y * y, axis=-1, keepdims=True) + EPS) * g_ref[...]


def _ffn(hf, x1, wg, wu, wd, g_final, tm, tf, name):
    m, d = x1.shape
    dff = wg.shape[1]
    return pl.pallas_call(
        _ffn_body,
        grid=(m // tm, dff // tf),
        in_specs=[pl.BlockSpec((tm, d), lambda i, f: (i, 0)), pl.BlockSpec((tm, d), lambda i, f: (i, 0)),
                  pl.BlockSpec((d, tf), lambda i, f: (0, f)), pl.BlockSpec((d, tf), lambda i, f: (0, f)),
                  pl.BlockSpec((tf, d), lambda i, f: (f, 0)), _resident((1, d))],
        out_specs=pl.BlockSpec((tm, d), lambda i, f: (i, 0)),
        out_shape=jax.ShapeDtypeStruct((m, d), F32),
        compiler_params=_params(("arbitrary", "arbitrary")),
        name=name,
    )(hf, x1, wg, wu, wd, g_final)


class _Decode:
    def __init__(self, pt_ref, in_refs, oa_ref, scratch, *, ppc, nch, total, n_heads, dh, rh):
        (self.q, self.kn, self.vn, self.lfn, self.g, self.uv, self.slt, self.ck, self.cv, self.clf) = in_refs
        (self.kbuf, self.vbuf, self.lfbuf, self.sbuf, self.sem, self.m, self.l, self.acc, self.carry, self.cfin) = scratch
        self.pt, self.oa = pt_ref, oa_ref
        self.ppc, self.nch, self.total, self.n_heads, self.dh, self.rh = ppc, nch, total, n_heads, dh, rh
        self.lf_rows = rh // LANES

    def key_copies(self, t, slot):
        out = []
        for j in range(self.ppc):
            pid = self.pt[t * self.ppc + j]
            out.append(pltpu.make_async_copy(self.ck.at[pid], self.kbuf.at[slot, pl.ds(j * self.rh, self.rh), :],
                                             self.sem.at[0, slot]))
            out.append(pltpu.make_async_copy(self.clf.at[pid],
                                             self.lfbuf.at[slot, pl.ds(j * self.lf_rows, self.lf_rows), :],
                                             self.sem.at[2, slot]))
        return out

    def value_copies(self, t, slot):
        out = []
        for j in range(self.ppc):
            pid = self.pt[t * self.ppc + j]
            out.append(pltpu.make_async_copy(self.cv.at[pid], self.vbuf.at[slot, pl.ds(j * self.rh, self.rh), :],
                                             self.sem.at[1, slot]))
        return out

    def _bias_and_mask(self, slot):
        n_heads = self.n_heads
        nrow = self.ppc * self.lf_rows
        lfv = self.lfbuf[slot]
        r = jnp.dot(jnp.concatenate(_split3(lfv), axis=0), self.uv[...], preferred_element_type=F32)
        r = r[0:nrow] + r[nrow:2 * nrow] + r[2 * nrow:3 * nrow]
        within, rowtot = r[:, :LANES], r[:, LANES:]
        rr = jnp.dot(self.slt[...], jnp.concatenate(_split3(rowtot), axis=1), preferred_element_type=F32)
        offs = rr[:, :LANES] + rr[:, LANES:2 * LANES] + rr[:, 2 * LANES:]
        carry = self.carry[...]
        cs = within + offs + carry
        self.carry[...] = carry + offs[nrow - 1:nrow, :] + rowtot[nrow - 1:nrow, :]
        return jnp.concatenate([jnp.broadcast_to(cs[i:i + 1, :], (n_heads, LANES)) for i in range(nrow)], axis=1)

    def _head_match(self):
        lane = lax.broadcasted_iota(jnp.int32, (self.n_heads, LANES), 1)
        return lane, jnp.bitwise_and(lane, self.n_heads - 1) == lax.broadcasted_iota(jnp.int32, (self.n_heads, LANES), 0)

    def _scores(self, t, slot):
        q = self.q[t // self.nch]
        s = lax.dot_general(q, self.kbuf[slot], _NT, preferred_element_type=F32)
        self.sbuf[slot] = s - self._bias_and_mask(slot)

    def _update(self, s, valid, v_rows):
        s = jnp.where(valid, s, -jnp.inf)
        m_old = self.m[...]
        m_new = jnp.maximum(m_old, jnp.max(s, axis=-1, keepdims=True))
        alpha = jnp.exp(m_old - m_new)
        pr = jnp.exp(s - m_new)
        self.l[...] = alpha * self.l[...] + jnp.sum(pr, axis=-1, keepdims=True)
        self.acc[...] = alpha * self.acc[...] + jnp.dot(pr, v_rows, preferred_element_type=F32)
        self.m[...] = m_new

    def _reset_softmax(self):
        self.m[...] = jnp.full(self.m.shape, -jnp.inf, F32)
        self.l[...] = jnp.zeros(self.l.shape, F32)
        self.acc[...] = jnp.zeros(self.acc.shape, F32)

    def _finish_batch(self, b):
        n_heads, dh = self.n_heads, self.dh
        lane, head_match = self._head_match()
        q = self.q[b]
        zpad = jnp.zeros((LANES - n_heads, dh), F32)
        k_self = jnp.concatenate([zpad, self.kn[b]], axis=0)
        v_self = jnp.concatenate([zpad, self.vn[b]], axis=0)
        s_self = lax.dot_general(q, k_self, _NT, preferred_element_type=F32)
        c_t = self.cfin[...] + self.lfn[b]
        self._update(s_self - c_t, jnp.logical_and(lane >= LANES - n_heads, head_match), v_self)
        o = self.acc[...] / self.l[...]
        ms = jnp.sum(jnp.sum(o * o, axis=-1, keepdims=True), axis=0, keepdims=True) / (n_heads * dh)
        self.oa[b] = o * lax.rsqrt(ms + EPS) * self.g[...]

    def start_first(self):
        for t in range(min(2, self.total)):
            for cp in self.key_copies(t, t) + self.value_copies(t, t):
                cp.start()
        for cp in self.key_copies(0, 0):
            cp.wait()
        self.carry[...] = jnp.zeros(self.carry.shape, F32)
        self._scores(0, 0)

        @pl.when(2 < self.total)
        def _():
            for cp in self.key_copies(2, 0):
                cp.start()

    def call(self, u, with_keys):
        nch = self.nch
        tv = u - 1
        bv = tv // nch
        cv = tv - bv * nch
        vslot = lax.rem(tv, 2)
        kslot = 1 - vslot
        if with_keys:
            for cp in self.key_copies(u, kslot):
                cp.wait()
        for cp in self.value_copies(tv, vslot):
            cp.wait()

        first_of_batch = cv == nch - 1 if with_keys else True

        @pl.when(first_of_batch)
        def _():
            self.cfin[...] = self.carry[...]
            self.carry[...] = jnp.zeros(self.carry.shape, F32)

        @pl.when(cv == 0)
        def _():
            self._reset_softmax()

        s_prev = self.sbuf[vslot]
        if with_keys:
            self._scores(u, kslot)
        _, head_match = self._head_match()
        valid = jnp.concatenate([head_match] * (self.ppc * self.lf_rows), axis=1)
        self._update(s_prev, valid, self.vbuf[vslot])

        if with_keys:
            @pl.when(u + 2 < self.total)
            def _():
                for cp in self.key_copies(u + 2, kslot):
                    cp.start()

        @pl.when(u + 1 < self.total)
        def _():
            for cp in self.value_copies(u + 1, vslot):
                cp.start()

        @pl.when(cv == nch - 1)
        def _():
            self._finish_batch(bv)


def _decode_consts(nrow, n_heads):
    li = np.arange(LANES)
    same_head = (li[:, None] % n_heads) == (li[None, :] % n_heads)
    u = same_head & (li[:, None] <= li[None, :])
    ri = np.arange(nrow)
    slt = ri[None, :] < ri[:, None]
    as_bf16 = lambda a: jnp.asarray(a.astype(np.float32), BF16)
    return as_bf16(np.concatenate([u, same_head], axis=1)), as_bf16(slt)


def _ffn_decode_body(pt_ref, hf_ref, x1_ref, wg_ref, wu_ref, wd_ref, g_ref, *refs, cps, dcfg):
    dec_in, (o_ref, oa_ref), scratch = refs[:10], refs[10:12], refs[12:]
    act_ref, scratch = scratch[0], scratch[1:]
    dec = _Decode(pt_ref, dec_in, oa_ref, scratch, **dcfg)
    f = pl.program_id(1)
    nf = pl.num_programs(1)
    step = pl.program_id(0) * nf + f
    tf = wg_ref.shape[1]
    n_sub = tf // FFN_SUB

    @pl.when(step == 0)
    def _():
        dec.start_first()

    @pl.when(f == 0)
    def _():
        o_ref[...] = x1_ref[...]

    def gate_up(j):
        sl = slice(j * FFN_SUB, (j + 1) * FFN_SUB)
        hf = hf_ref[...]
        gate = jnp.dot(hf, wg_ref[:, sl], preferred_element_type=F32)
        up = jnp.dot(hf, wu_ref[:, sl], preferred_element_type=F32)
        act_ref[:, sl] = (_silu(gate) * up).astype(BF16)

    def down():
        d = o_ref.shape[1]
        for n0 in range(0, d, 512):
            o_ref[:, n0:n0 + 512] += jnp.dot(act_ref[...], wd_ref[:, n0:n0 + 512], preferred_element_type=F32)

    pieces = [functools.partial(gate_up, j) for j in range(n_sub)] + [down]
    per_piece = [cps // len(pieces) + (1 if i < cps % len(pieces) else 0) for i in range(len(pieces))]
    tail_step, tail_j = (dec.total - 1) // cps, (dec.total - 1) % cps
    j = 0
    for piece, n_calls in zip(pieces, per_piece):
        for _ in range(n_calls):
            u = step * cps + j + 1

            @pl.when(u < dec.total)
            def _(u=u):
                dec.call(u, True)

            if j == tail_j:
                @pl.when(step == tail_step)
                def _(u=u):
                    dec.call(u, False)

            j += 1
        piece()

    @pl.when(f == nf - 1)
    def _():
        y = o_ref[...]
        o_ref[...] = y * lax.rsqrt(jnp.mean(y * y, axis=-1, keepdims=True) + EPS) * g_ref[...]


def _ffn_decode(hf, x1, wg, wu, wd, g_final, tm, tf, page_table, q, k_new, v_new, lf_new, g_attn,
                cache_k, cache_v, cache_lf, ppc):
    m, d = x1.shape
    dff = wg.shape[1]
    nb, n_heads, dh = q.shape
    n_pages = page_table.shape[1]
    rh = cache_k.shape[1]
    nch = n_pages // ppc
    total = nb * nch
    lf_rows = rh // LANES
    uv, slt = _decode_consts(ppc * lf_rows, n_heads)
    n_steps = (m // tm) * (dff // tf)
    cps = -(-total // n_steps)
    dcfg = dict(ppc=ppc, nch=nch, total=total, n_heads=n_heads, dh=dh, rh=rh)
    whole = lambda a: pl.BlockSpec(a.shape, lambda i, f, pt: (0,) * a.ndim)
    any_spec = pl.BlockSpec(memory_space=pl.ANY)
    dec_args = (q, k_new, v_new, lf_new, g_attn, uv, slt)
    grid_spec = pltpu.PrefetchScalarGridSpec(
        num_scalar_prefetch=1,
        grid=(m // tm, dff // tf),
        in_specs=[pl.BlockSpec((tm, d), lambda i, f, pt: (i, 0), pipeline_mode=pl.Buffered(1)),
                  pl.BlockSpec((tm, d), lambda i, f, pt: (i, 0), pipeline_mode=pl.Buffered(1)),
                  pl.BlockSpec((d, tf), lambda i, f, pt: (0, f)), pl.BlockSpec((d, tf), lambda i, f, pt: (0, f)),
                  pl.BlockSpec((tf, d), lambda i, f, pt: (f, 0)), whole(g_final)]
        + [whole(a) for a in dec_args] + [any_spec] * 3,
        out_specs=[pl.BlockSpec((tm, d), lambda i, f, pt: (i, 0), pipeline_mode=pl.Buffered(1)),
                   pl.BlockSpec((nb, n_heads, dh), lambda i, f, pt: (0, 0, 0))],
        scratch_shapes=[
            pltpu.VMEM((tm, tf), BF16),
            pltpu.VMEM((2, ppc * rh, dh), F32), pltpu.VMEM((2, ppc * rh, dh), F32),
            pltpu.VMEM((2, ppc * lf_rows, LANES), F32),
            pltpu.VMEM((2, n_heads, ppc * rh), F32),
            pltpu.SemaphoreType.DMA((3, 2)),
            pltpu.VMEM((n_heads, 1), F32), pltpu.VMEM((n_heads, 1), F32), pltpu.VMEM((n_heads, dh), F32),
            pltpu.VMEM((1, LANES), F32), pltpu.VMEM((1, LANES), F32),
        ],
    )
    return pl.pallas_call(
        functools.partial(_ffn_decode_body, cps=cps, dcfg=dcfg),
        grid_spec=grid_spec,
        out_shape=[jax.ShapeDtypeStruct((m, d), F32), jax.ShapeDtypeStruct((nb, n_heads, dh), F32)],
        compiler_params=_params(("arbitrary", "arbitrary")),
        name="ffn_with_paged_decode",
    )(page_table.reshape(-1), hf, x1, wg, wu, wd, g_final, *dec_args, cache_k, cache_v, cache_lf)


def _ssd_step_body(xbc_ref, sc_ref, z_ref, dt_ref, st_ref, cw_ref, cb_ref, arow_ref, e_ref, dsk_ref, g_ref,
                   o_ref, sto_ref, sco_ref, *, ds, n_groups, n_state, n_heads, p, hoff, kw):
    xn = xbc_ref[...]
    a = cw_ref[0:1, :] * sc_ref[0:1, :]
    for i in range(1, kw - 1):
        a = a + cw_ref[i:i + 1, :] * sc_ref[i:i + 1, :]
    a = cb_ref[...] + (a + cw_ref[kw - 1:kw, :] * xn)
    act = _silu(a)
    for i in range(kw - 2):
        sco_ref[i:i + 1, :] = sc_ref[i + 1:i + 2, :]
    sco_ref[kw - 2:kw - 1, :] = xn

    dt = dt_ref[...]
    da = dt * arow_ref[...]
    dt_exp = _dot_r01(jnp.broadcast_to(dt, (SUBLANES, LANES)), e_ref[...])[0:1, :]
    decay = jnp.exp(da)
    xs = act[:, :ds]
    xdt = xs * dt_exp
    hpg = n_heads // n_groups
    gw = hpg * p
    eye = lax.broadcasted_iota(jnp.int32, (gw, gw), 0) == lax.broadcasted_iota(jnp.int32, (gw, gw), 1)
    y_parts = []
    for g in range(n_groups):
        bg = act[:, ds + g * n_state:ds + (g + 1) * n_state]
        cg = act[:, ds + (n_groups + g) * n_state:ds + (n_groups + g + 1) * n_state]
        diag = jnp.where(eye, jnp.broadcast_to(xdt[:, g * gw:(g + 1) * gw], (gw, gw)), 0.0).astype(BF16)
        outer = jnp.dot(diag, jnp.broadcast_to(bg, (gw, n_state)).astype(BF16), preferred_element_type=F32)
        for jj in range(hpg):
            j = g * hpg + jj
            rows = slice(j * p, (j + 1) * p)
            sto_ref[rows, :] = st_ref[rows, :] * decay[:, hoff + j:hoff + j + 1] + outer[jj * p:(jj + 1) * p, :]
        st_new = sto_ref[g * gw:(g + 1) * gw, :].astype(BF16)
        c8 = jnp.broadcast_to(cg, (SUBLANES, n_state)).astype(BF16)
        y_parts.append(lax.dot_general(c8, st_new, _NT, preferred_element_type=F32)[0:1, :])
    y = jnp.concatenate(y_parts, axis=1)
    y = (y + dsk_ref[...] * xs) * _silu(z_ref[...])
    o_ref[...] = (y * lax.rsqrt(jnp.mean(y * y, axis=-1, keepdims=True) + EPS) * g_ref[...]).astype(o_ref.dtype)


def _ssd_step(xbc, state_conv, z, dt, state, conv_w, conv_b, a_row, expand, dskip_row, g_ssd, dims):
    ds, n_groups, n_state, n_heads, p, hoff, kw = dims
    nb, _, dc = xbc.shape
    per_b = lambda b: (b, 0, 0)
    body = functools.partial(_ssd_step_body, ds=ds, n_groups=n_groups, n_state=n_state, n_heads=n_heads,
                             p=p, hoff=hoff, kw=kw)
    return pl.pallas_call(
        body,
        grid=(nb,),
        in_specs=[
            pl.BlockSpec((None, 1, dc), per_b), pl.BlockSpec((None, kw - 1, dc), per_b),
            pl.BlockSpec((None, 1, ds), per_b), pl.BlockSpec((None, 1, LANES), per_b),
            pl.BlockSpec((None, n_heads * p, n_state), per_b),
            _resident((kw, dc)), _resident((1, dc)), _resident((1, LANES)), _resident(expand.shape),
            _resident((1, ds)), _resident((1, ds)),
        ],
        out_specs=[pl.BlockSpec((None, 1, ds), per_b), pl.BlockSpec((None, n_heads * p, n_state), per_b),
                   pl.BlockSpec((None, kw - 1, dc), per_b)],
        out_shape=[jax.ShapeDtypeStruct((nb, 1, ds), BF16), jax.ShapeDtypeStruct(state.shape, F32),
                   jax.ShapeDtypeStruct(state_conv.shape, F32)],
        compiler_params=_params(("arbitrary",)),
        name="ssd_step",
    )(xbc, state_conv, z, dt, state, conv_w, conv_b, a_row, expand, dskip_row, g_ssd)


def _pick_tile(m, pref):
    t = min(m, pref)
    while m % t:
        t //= 2
    return t


def kernel(x_prompt, x_sample, cache_k, cache_v, cache_logf, state_ssm, state_conv, page_table, g_mix, w_in, b_fgate, conv_w, conv_b, dt_bias, a_log, d_skip, g_attn, g_ssd, w_out, g_ffn, w_gate, w_up, w_down, g_final):
    bp, seq, d = x_prompt.shape
    bs = x_sample.shape[0]
    depth, n_pool, page, n_heads, dh = cache_k.shape
    da = n_heads * dh
    nh, p, n_state = state_ssm.shape[2:]
    ds = nh * p
    kw = conv_w.shape[1]
    dc = conv_w.shape[2]
    n_groups = (dc - ds) // (2 * n_state)
    hoff = n_heads
    assert depth == 1, "the FFN kernel applies the final rmsnorm, so it must be the last layer"
    assert x_sample.shape[1] == 1 and seq % SSD_CHUNK == 0 and n_heads + nh <= LANES - n_heads
    assert n_heads & (n_heads - 1) == 0
    assert (page * n_heads) % LANES == 0 and p * 2 == LANES and (nh // n_groups) % 2 == 0
    dims = (ds, n_groups, n_state, nh, p, hoff, kw)
    o_f, o_z, o_x, o_dt = 3 * da, 3 * da + n_heads, 3 * da + n_heads + ds, 3 * da + n_heads + ds + dc
    mp = bp * seq

    expand_np = np.zeros((LANES, ds), np.float32)
    for j in range(nh):
        expand_np[hoff + j, j * p:(j + 1) * p] = 1.0
    expand = jnp.asarray(expand_np, BF16)

    yp = x_prompt.reshape(mp, d)
    ys = x_sample.reshape(bs, d)
    outs = [[] for _ in range(10)]
    for l in range(depth):
        w = w_in[l]
        w_q, w_k, w_v = (w[:, i * da:(i + 1) * da].astype(BF16) for i in range(3))
        w_f = w[:, o_f:o_f + n_heads]
        w_fd = jnp.concatenate([w_f, w[:, o_dt:o_dt + nh], jnp.zeros((d, LANES - 2 * n_heads - nh), w.dtype), w_f],
                               axis=1).astype(BF16)
        w_z = w[:, o_z:o_z + ds].astype(BF16)
        w_x = w[:, o_x:o_x + dc].astype(BF16)
        w_oa = w_out[l][:da].astype(BF16)
        w_os = w_out[l][da:].astype(BF16)
        wg, wu, wd = w_gate[l].astype(BF16), w_up[l].astype(BF16), w_down[l].astype(BF16)
        zpad = jnp.zeros((LANES - 2 * n_heads - nh,), F32)
        bias_row = jnp.concatenate([b_fgate[l], dt_bias[l], zpad, b_fgate[l]]).reshape(1, LANES)
        a_row = jnp.concatenate([jnp.zeros((n_heads,), F32), -jnp.exp(a_log[l].astype(F32)),
                                 jnp.zeros((LANES - n_heads - nh,), F32)]).reshape(1, LANES)
        dskip_row = jnp.repeat(d_skip[l].astype(F32), p).reshape(1, ds)
        gm, ga, gs, gf = (g_mix[l].reshape(1, d), g_attn[l].reshape(1, da), g_ssd[l].reshape(1, ds),
                          g_ffn[l].reshape(1, d))
        gfin = g_final.reshape(1, d)
        cw, cb = conv_w[l].astype(F32), conv_b[l].reshape(1, dc).astype(F32)
        proj_a = dict(weights=[w_q, w_k, w_v, w_fd], out_map=[[0], [1, 3], [2, 4], [5]],
                      out_dtypes=[BF16, F32, F32, BF16, BF16, F32])
        proj_b = dict(weights=[w_z, w_x], out_map=[[0], [1]], out_dtypes=[F32, F32])

        tm = _pick_tile(mp, 512)
        q, k, v, kb, vb, fd = _norm_proj(yp, gm, tm=tm, name="in_proj_qkv", out_scale={0: dh ** -0.5 * LOG2E}, **proj_a)
        z, xbc = _norm_proj(yp, gm, tm=_pick_tile(mp, 256), name="in_proj_ssd", **proj_b)
        lf, dt, cs, cst = _gates(fd, bias_row, a_row, bp, seq, n_heads)
        oa = _attention(q, kb, vb, cst, ga, bp, seq, n_heads, dh, tq=_pick_tile(seq, 256))
        osm, st_p = _ssd_prompt(xbc, z, dt, cs, cst, cw, cb, dskip_row, gs, bp, seq, dims)
        x1, hf = _outproj(oa, osm, yp, w_oa, w_os, gf, tm, "out_proj")

        qs, k_s, v_s, _, _, fds = _norm_proj(ys, gm, tm=bs, name="in_proj_qkv_s", out_scale={0: dh ** -0.5}, **proj_a)
        zs, xbcs = _norm_proj(ys, gm, tm=bs, name="in_proj_ssd_s", **proj_b)
        lfs, dts = _sgates(fds, bias_row)
        ck = cache_k[l].reshape(n_pool, page * n_heads, dh)
        cv = cache_v[l].reshape(n_pool, page * n_heads, dh)
        clf = cache_logf[l].astype(F32).reshape(n_pool, page * n_heads // LANES, LANES)
        dff = wg.shape[1]
        tf = _pick_tile(dff, 512)
        yp_new, oas = _ffn_decode(hf, x1, wg, wu, wd, gfin, _pick_tile(mp, 1024), tf, page_table,
                                  qs.astype(F32).reshape(bs, n_heads, dh), k_s.reshape(bs, n_heads, dh),
                                  v_s.reshape(bs, n_heads, dh), lfs.reshape(bs, 1, LANES), ga.reshape(n_heads, dh),
                                  ck, cv, clf, ppc=_pick_tile(page_table.shape[1], DECODE_PAGES_PER_CHUNK))
        oss, st_s, conv_s = _ssd_step(xbcs.reshape(bs, 1, dc), state_conv[l].astype(F32), zs.reshape(bs, 1, ds),
                                      dts.reshape(bs, 1, LANES), state_ssm[l].astype(F32).reshape(bs, nh * p, n_state),
                                      cw, cb, a_row, expand, dskip_row, gs, dims)
        x1s, hfs = _outproj(oas.astype(BF16).reshape(bs, da), oss.reshape(bs, ds), ys, w_oa, w_os, gf, bs, "out_proj_s")
        ys_new = _ffn(hfs, x1s, wg, wu, wd, gfin, bs, tf, "ffn_s")

        yp, ys = yp_new, ys_new
        new = [k.reshape(bp, seq, n_heads, dh), v.reshape(bp, seq, n_heads, dh),
               lf[:, :n_heads].reshape(bp, seq, n_heads),
               k_s.reshape(bs, 1, n_heads, dh), v_s.reshape(bs, 1, n_heads, dh),
               lfs[:, :n_heads].reshape(bs, 1, n_heads),
               st_p.reshape(bp, nh, p, n_state), xbc.reshape(bp, seq, dc)[:, seq - kw + 1:],
               st_s.reshape(bs, nh, p, n_state), conv_s]
        for acc, val in zip(outs, new):
            acc.append(val)

    return (yp.reshape(bp, seq, d), ys.reshape(bs, 1, d)) + tuple(jnp.stack(o) for o in outs)
```

```python
import functools

import numpy as np
import jax
import jax.numpy as jnp
from jax import lax
from jax.experimental import pallas as pl
from jax.experimental.pallas import tpu as pltpu

F32 = jnp.float32
BF16 = jnp.bfloat16
EPS = 1e-6
LOG2E = 1.4426950408889634
LANES = 128
SUBLANES = 8
SSD_CHUNK = 128
VMEM_LIMIT_MB = 56
FFN_SUB = 256
DECODE_PAGES_PER_CHUNK = 8
DECODE_ATTENTION_SHARE = 4

_NT = (((1,), (1,)), ((), ()))
_TN = (((0,), (0,)), ((), ()))


def _params(sem):
    return pltpu.CompilerParams(dimension_semantics=sem, vmem_limit_bytes=VMEM_LIMIT_MB * 2**20)


def _resident(shape):
    nd = len(shape)
    return pl.BlockSpec(shape, lambda *_: (0,) * nd, pipeline_mode=pl.Buffered(1))


def _silu(x):
    return x / (1.0 + jnp.exp(-x))


def _split3(x):
    hi = x.astype(BF16)
    r1 = x - hi.astype(F32)
    mid = r1.astype(BF16)
    lo = (r1 - mid.astype(F32)).astype(BF16)
    return hi, mid, lo


def _dot_r01(a, b01):
    hi, mid, lo = _split3(a)
    d = functools.partial(jnp.dot, preferred_element_type=F32)
    return d(hi, b01) + d(mid, b01) + d(lo, b01)


def _dot_l01(a01, b):
    hi, mid, lo = _split3(b)
    d = functools.partial(jnp.dot, preferred_element_type=F32)
    return d(a01, hi) + d(a01, mid) + d(a01, lo)


def _norm_proj_body(x_ref, g_ref, *refs, n_w, out_map, out_scale, tn):
    w_refs, o_refs = refs[:n_w], refs[n_w:]
    x = x_ref[...]
    h = (x * lax.rsqrt(jnp.mean(x * x, axis=-1, keepdims=True) + EPS) * g_ref[...]).astype(BF16)
    for wi, w_ref in enumerate(w_refs):
        n = w_ref.shape[1]
        for n0 in range(0, n, tn):
            n1 = min(n0 + tn, n)
            r = jnp.dot(h, w_ref[:, n0:n1], preferred_element_type=F32)
            for oi in out_map[wi]:
                o_refs[oi][:, n0:n1] = (r * out_scale[oi] if oi in out_scale else r).astype(o_refs[oi].dtype)


def _norm_proj(x, g, weights, out_map, out_dtypes, tm, name, out_scale=None):
    m, d = x.shape
    widths = {}
    for wi, ois in enumerate(out_map):
        for oi in ois:
            widths[oi] = weights[wi].shape[1]
    n_out = len(out_dtypes)
    body = functools.partial(_norm_proj_body, n_w=len(weights), out_map=out_map, out_scale=out_scale or {}, tn=512)
    return pl.pallas_call(
        body,
        grid=(m // tm,),
        in_specs=[pl.BlockSpec((tm, d), lambda i: (i, 0)), _resident((1, d))]
        + [_resident(w.shape) for w in weights],
        out_specs=[pl.BlockSpec((tm, widths[oi]), lambda i: (i, 0)) for oi in range(n_out)],
        out_shape=[jax.ShapeDtypeStruct((m, widths[oi]), out_dtypes[oi]) for oi in range(n_out)],
        compiler_params=_params(("arbitrary",)),
        name=name,
    )(x, g, *weights)


def _gates_body(fd_ref, bias_ref, arow_ref, lf_ref, dt_ref, cs_ref, cst_ref, *, n_heads):
    t = fd_ref.shape[0]
    L = SSD_CHUNK
    lane = lax.broadcasted_iota(jnp.int32, (L, LANES), 1)
    tri = (lax.broadcasted_iota(jnp.int32, (L, L), 0) >= lax.broadcasted_iota(jnp.int32, (L, L), 1)).astype(BF16)
    is_f = lane < n_heads
    carry = jnp.zeros((1, LANES), F32)
    for c in range(t // L):
        rows = slice(c * L, (c + 1) * L)
        v = fd_ref[rows, :] + bias_ref[...]
        sp = jnp.log1p(jnp.exp(-jnp.abs(v)))
        lf = jnp.minimum(v, 0.0) - sp
        dt = jnp.maximum(v, 0.0) + sp
        zc = jnp.where(is_f, lf, dt * arow_ref[...])
        cs = _dot_l01(tri, zc) + jnp.where(is_f, carry, 0.0)
        carry = cs[L - 1:L, :]
        lf_ref[rows, :] = lf
        dt_ref[rows, :] = dt
        cs_ref[rows, :] = cs
        cst_ref[:, rows] = cs.T


def _gates(fd, bias_row, a_row, batch, seq, n_heads):
    m = fd.shape[0]
    row = lambda b: (b, 0)
    return pl.pallas_call(
        functools.partial(_gates_body, n_heads=n_heads),
        grid=(batch,),
        in_specs=[pl.BlockSpec((seq, LANES), row), _resident((1, LANES)), _resident((1, LANES))],
        out_specs=[pl.BlockSpec((seq, LANES), row)] * 3 + [pl.BlockSpec((None, LANES, seq), lambda b: (b, 0, 0))],
        out_shape=[jax.ShapeDtypeStruct((m, LANES), F32)] * 3 + [jax.ShapeDtypeStruct((batch, LANES, seq), F32)],
        compiler_params=_params(("arbitrary",)),
        name="gates",
    )(fd, bias_row, a_row)


def _sgates_body(fd_ref, bias_ref, lf_ref, dt_ref):
    v = fd_ref[...] + bias_ref[...]
    sp = jnp.log1p(jnp.exp(-jnp.abs(v)))
    lf_ref[...] = jnp.minimum(v, 0.0) - sp
    dt_ref[...] = jnp.maximum(v, 0.0) + sp


def _sgates(fd, bias_row):
    return pl.pallas_call(
        _sgates_body,
        out_shape=[jax.ShapeDtypeStruct(fd.shape, F32)] * 2,
        name="sample_gates",
    )(fd, bias_row)


def _attn_body(pt_ref, q_ref, k_ref, v_ref, ct_ref, g_ref, *refs, tq, n_heads, dh, cps, dcfg):
    dec_in, (o_ref, oa_ref), scratch = refs[:N_DECODE_IN], refs[N_DECODE_IN:N_DECODE_IN + 2], refs[N_DECODE_IN + 2:]
    (m_ref, l_ref, acc_ref), dec_scratch = scratch[:3], scratch[3:]
    qi = pl.program_id(1)

    dec = _Decode(pt_ref, dec_in, oa_ref, dec_scratch, **dcfg)
    step = pl.program_id(0) * pl.num_programs(1) + qi

    @pl.when(step == 0)
    def _():
        dec.start_first()

    _emit_decode_calls(dec, step, cps, range(cps))

    causal = lax.broadcasted_iota(jnp.int32, (tq, tq), 0) >= lax.broadcasted_iota(jnp.int32, (tq, tq), 1)
    m_ref[...] = jnp.full(m_ref.shape, -jnp.inf, F32)
    l_ref[...] = jnp.zeros(l_ref.shape, F32)
    acc_ref[...] = jnp.zeros(acc_ref.shape, F32)

    def kv_block(kj, masked):
        k0 = pl.multiple_of(kj * tq, tq)
        for h in range(n_heads):
            sl = slice(h * dh, (h + 1) * dh)
            s = lax.dot_general(q_ref[:, sl], k_ref[pl.ds(k0, tq), sl], _NT, preferred_element_type=F32)
            s = s - ct_ref[h:h + 1, pl.ds(k0, tq)] * LOG2E
            if masked:
                s = jnp.where(causal, s, -jnp.inf)
            m_old = m_ref[h]
            m_new = jnp.maximum(m_old, jnp.max(s, axis=-1, keepdims=True))
            alpha = jnp.exp2(m_old - m_new)
            p = jnp.exp2(s - jnp.concatenate([m_new] * (tq // LANES), axis=1))
            psum = p[:, :LANES]
            for c0 in range(LANES, tq, LANES):
                psum = psum + p[:, c0:c0 + LANES]
            l_ref[h] = alpha * l_ref[h] + psum
            acc_ref[:, sl] = alpha * acc_ref[:, sl] + jnp.dot(p.astype(BF16), v_ref[pl.ds(k0, tq), sl],
                                                              preferred_element_type=F32)
            m_ref[h] = m_new

    def loop_body(kj, carry):
        kv_block(kj, False)
        return carry

    lax.fori_loop(0, qi, loop_body, 0)
    kv_block(qi, True)
    for h in range(n_heads):
        sl = slice(h * dh, (h + 1) * dh)
        acc_ref[:, sl] = acc_ref[:, sl] / jnp.sum(l_ref[h], axis=-1, keepdims=True)
    o = acc_ref[...]
    o_ref[...] = (o * lax.rsqrt(jnp.mean(o * o, axis=-1, keepdims=True) + EPS) * g_ref[...]).astype(o_ref.dtype)


def _attention(q, kb, vb, cst, g_attn, batch, seq, n_heads, dh, tq, decode_operands, b0, b1):
    m, da = q.shape
    nq = seq // tq
    static, dec_args, dec_specs, oa_spec, oa_shape, dec_scratch = _decode_plumbing(batch * nq, *decode_operands, b0, b1)
    body = functools.partial(_attn_body, tq=tq, n_heads=n_heads, dh=dh, **static)
    grid_spec = pltpu.PrefetchScalarGridSpec(
        num_scalar_prefetch=1,
        grid=(batch, nq),
        in_specs=[
            pl.BlockSpec((tq, da), lambda b, i, pt: (b * nq + i, 0)),
            pl.BlockSpec((seq, da), lambda b, i, pt: (b, 0)),
            pl.BlockSpec((seq, da), lambda b, i, pt: (b, 0)),
            pl.BlockSpec((None, SUBLANES, seq), lambda b, i, pt: (b, 0, 0)),
            pl.BlockSpec((1, da), lambda b, i, pt: (0, 0)),
        ] + dec_specs,
        out_specs=[pl.BlockSpec((tq, da), lambda b, i, pt: (b * nq + i, 0)), oa_spec],
        scratch_shapes=[pltpu.VMEM((n_heads, tq, LANES), F32), pltpu.VMEM((n_heads, tq, LANES), F32),
                        pltpu.VMEM((tq, da), F32)] + dec_scratch,
    )
    return pl.pallas_call(
        body,
        grid_spec=grid_spec,
        out_shape=[jax.ShapeDtypeStruct((m, da), BF16), oa_shape],
        compiler_params=_params(("arbitrary", "arbitrary")),
        name="fox_attention_with_paged_decode",
    )(decode_operands[0].reshape(-1), q, kb, vb, cst, g_attn, *dec_args)


def _ssd_body(xbc_ref, z_ref, dt_ref, cs_ref, cst_ref, cw_ref, cb_ref, dsk_ref, g_ref,
              o_ref, st_ref, xp_ref, act_ref, y_ref, *, ds, n_groups, n_state, n_heads, p, hoff, kw):
    L = SSD_CHUNK
    dc = xbc_ref.shape[1]
    c = pl.program_id(1)
    pad = SUBLANES

    @pl.when(c == 0)
    def _():
        st_ref[...] = jnp.zeros_like(st_ref)
        xp_ref[:, 0:pad, :] = jnp.zeros((dc // LANES, pad, LANES), F32)

    slab = 512
    for t in range(dc // LANES):
        sl = slice(t * LANES, (t + 1) * LANES)
        xp_ref[t, pad:pad + L, :] = xbc_ref[:, sl]
        a = cw_ref[0:1, sl] * xp_ref[t, pad - kw + 1:pad - kw + 1 + L, :]
        for i in range(1, kw):
            a = a + cw_ref[i:i + 1, sl] * xp_ref[t, pad - kw + 1 + i:pad - kw + 1 + i + L, :]
        a = cb_ref[:, sl] + a
        act_ref[:, sl] = _silu(a)
        xp_ref[t, 0:pad, :] = xp_ref[t, L:L + pad, :]

    dtv = dt_ref[...]
    cumv = cs_ref[...]
    cumt = cst_ref[...]
    lane = lax.broadcasted_iota(jnp.int32, (L, LANES), 1)
    causal = lax.broadcasted_iota(jnp.int32, (L, L), 0) >= lax.broadcasted_iota(jnp.int32, (L, L), 1)
    first = lane < p
    hpg = n_heads // n_groups
    gw = hpg * p
    for g in range(n_groups):
        bg = act_ref[:, ds + g * n_state:ds + (g + 1) * n_state].astype(BF16)
        cg = act_ref[:, ds + (n_groups + g) * n_state:ds + (n_groups + g + 1) * n_state].astype(BF16)
        cbm = lax.dot_general(cg, bg, _NT, preferred_element_type=F32)
        st_g = st_ref[g * gw:(g + 1) * gw, :]
        yoff = lax.dot_general(cg, st_g.astype(BF16), _NT, preferred_element_type=F32)
        xd_parts = []
        for pr in range(hpg // 2):
            j0 = g * hpg + 2 * pr
            la, lb = hoff + j0, hoff + j0 + 1
            c0v, c1v = cumv[:, la:la + 1], cumv[:, lb:lb + 1]
            d0 = jnp.exp(jnp.where(causal, c0v - cumt[la:la + 1, :], -jnp.inf))
            d1 = jnp.exp(jnp.where(causal, c1v - cumt[lb:lb + 1, :], -jnp.inf))
            mm = jnp.concatenate([(cbm * d0).astype(BF16), (cbm * d1).astype(BF16)], axis=1)
            cols = slice(j0 * p, (j0 + 2) * p)
            xdt = act_ref[:, cols] * jnp.where(first, dtv[:, la:la + 1], dtv[:, lb:lb + 1])
            bd = jnp.concatenate([jnp.where(first, xdt, 0.0).astype(BF16),
                                  jnp.where(first, 0.0, xdt).astype(BF16)], axis=0)
            ydiag = jnp.dot(mm, bd, preferred_element_type=F32)
            cum_pair = jnp.where(first, c0v, c1v)
            y_ref[:, cols] = ydiag + yoff[:, 2 * pr * p:(2 * pr + 2) * p] * jnp.exp(cum_pair)
            last_pair = jnp.where(first[0:1, :], cumv[L - 1:L, la:la + 1], cumv[L - 1:L, lb:lb + 1])
            xd_parts.append((xdt * jnp.exp(last_pair - cum_pair)).astype(BF16))
        xd = jnp.concatenate(xd_parts, axis=1)
        snew = lax.dot_general(xd, bg, _TN, preferred_element_type=F32)
        for jj in range(hpg):
            j = g * hpg + jj
            cd = jnp.exp(cumv[L - 1:L, hoff + j:hoff + j + 1])
            rows = slice(j * p, (j + 1) * p)
            st_ref[rows, :] = st_ref[rows, :] * cd + snew[jj * p:(jj + 1) * p, :]

    ssq = jnp.zeros((L, 1), F32)
    for c0 in range(0, ds, slab):
        sl = slice(c0, c0 + slab)
        y = (y_ref[:, sl] + dsk_ref[:, sl] * act_ref[:, sl]) * _silu(z_ref[:, sl])
        y_ref[:, sl] = y
        ssq = ssq + jnp.sum(y * y, axis=-1, keepdims=True)
    rinv = lax.rsqrt(ssq / ds + EPS)
    for c0 in range(0, ds, slab):
        sl = slice(c0, c0 + slab)
        o_ref[:, sl] = (y_ref[:, sl] * rinv * g_ref[:, sl]).astype(o_ref.dtype)


def _ssd_prompt(xbc, z, dt, cs, cst, conv_w, conv_b, dskip_row, g_ssd, batch, seq, dims):
    ds, n_groups, n_state, n_heads, p, hoff, kw = dims
    m, dc = xbc.shape
    L = SSD_CHUNK
    nc = seq // L
    row = lambda b, c: (b * nc + c, 0)
    body = functools.partial(_ssd_body, ds=ds, n_groups=n_groups, n_state=n_state, n_heads=n_heads,
                             p=p, hoff=hoff, kw=kw)
    return pl.pallas_call(
        body,
        grid=(batch, nc),
        in_specs=[
            pl.BlockSpec((L, dc), row),
            pl.BlockSpec((L, ds), row),
            pl.BlockSpec((L, LANES), row),
            pl.BlockSpec((L, LANES), row),
            pl.BlockSpec((None, LANES, L), lambda b, c: (b, 0, c)),
            _resident((kw, dc)), _resident((1, dc)), _resident((1, ds)), _resident((1, ds)),
        ],
        out_specs=[pl.BlockSpec((L, ds), row),
                   pl.BlockSpec((None, n_heads * p, n_state), lambda b, c: (b, 0, 0))],
        out_shape=[jax.ShapeDtypeStruct((m, ds), BF16),
                   jax.ShapeDtypeStruct((batch, n_heads * p, n_state), F32)],
        scratch_shapes=[pltpu.VMEM((dc // LANES, L + SUBLANES, LANES), F32), pltpu.VMEM((L, dc), F32),
                        pltpu.VMEM((L, ds), F32)],
        compiler_params=_params(("arbitrary", "arbitrary")),
        name="ssd_prompt",
    )(xbc, z, dt, cs, cst, conv_w, conv_b, dskip_row, g_ssd)


def _outproj_body(oa_ref, os_ref, x_ref, wa_ref, ws_ref, g_ref, x1_ref, hf_ref, *, tn):
    d = x_ref.shape[1]
    oa, osm = oa_ref[...], os_ref[...]
    for n0 in range(0, d, tn):
        sl = slice(n0, n0 + tn)
        mix = jnp.dot(oa, wa_ref[:, sl], preferred_element_type=F32) + jnp.dot(osm, ws_ref[:, sl], preferred_element_type=F32)
        x1_ref[:, sl] = x_ref[:, sl] + mix
    x1 = x1_ref[...]
    hf_ref[...] = (x1 * lax.rsqrt(jnp.mean(x1 * x1, axis=-1, keepdims=True) + EPS) * g_ref[...]).astype(hf_ref.dtype)


def _outproj(oa, osm, x, wa, ws, g_ffn, tm, name):
    m, d = x.shape
    row = lambda i: (i, 0)
    return pl.pallas_call(
        functools.partial(_outproj_body, tn=512),
        grid=(m // tm,),
        in_specs=[pl.BlockSpec((tm, oa.shape[1]), row), pl.BlockSpec((tm, osm.shape[1]), row),
                  pl.BlockSpec((tm, d), row), _resident(wa.shape), _resident(ws.shape), _resident((1, d))],
        out_specs=[pl.BlockSpec((tm, d), row)] * 2,
        out_shape=[jax.ShapeDtypeStruct((m, d), F32), jax.ShapeDtypeStruct((m, d), BF16)],
        compiler_params=_params(("arbitrary",)),
        name=name,
    )(oa, osm, x, wa, ws, g_ffn)


def _ffn_body(hf_ref, x1_ref, wg_ref, wu_ref, wd_ref, g_ref, o_ref):
    f = pl.program_id(1)

    @pl.when(f == 0)
    def _():
        o_ref[...] = x1_ref[...]

    hf = hf_ref[...]
    tf = wg_ref.shape[1]
    acts = []
    for c0 in range(0, tf, FFN_SUB):
        sl = slice(c0, min(c0 + FFN_SUB, tf))
        gate = jnp.dot(hf, wg_ref[:, sl], preferred_element_type=F32)
        up = jnp.dot(hf, wu_ref[:, sl], preferred_element_type=F32)
        acts.append((_silu(gate) * up).astype(BF16))
    act = acts[0] if len(acts) == 1 else jnp.concatenate(acts, axis=1)
    part = jnp.dot(act, wd_ref[...], preferred_element_type=F32)

    o_ref[...] += part

    @pl.when(f == pl.num_programs(1) - 1)
    def _():
        y = o_ref[...]
        o_ref[...] = y * lax.rsqrt(jnp.mean(y * y, axis=-1, keepdims=True) + EPS) * g_ref[...]


def _ffn(hf, x1, wg, wu, wd, g_final, tm, tf, name):
    m, d = x1.shape
    dff = wg.shape[1]
    return pl.pallas_call(
        _ffn_body,
        grid=(m // tm, dff // tf),
        in_specs=[pl.BlockSpec((tm, d), lambda i, f: (i, 0)), pl.BlockSpec((tm, d), lambda i, f: (i, 0)),
                  pl.BlockSpec((d, tf), lambda i, f: (0, f)), pl.BlockSpec((d, tf), lambda i, f: (0, f)),
                  pl.BlockSpec((tf, d), lambda i, f: (f, 0)), _resident((1, d))],
        out_specs=pl.BlockSpec((tm, d), lambda i, f: (i, 0)),
        out_shape=jax.ShapeDtypeStruct((m, d), F32),
        compiler_params=_params(("arbitrary", "arbitrary")),
        name=name,
    )(hf, x1, wg, wu, wd, g_final)


class _Decode:
    def __init__(self, pt_ref, in_refs, oa_ref, scratch, *, ppc, nch, total, b0, n_heads, dh, rh):
        (self.q, self.kn, self.vn, self.lfn, self.g, self.uv, self.slt, self.ck, self.cv, self.clf) = in_refs
        (self.kbuf, self.vbuf, self.lfbuf, self.sbuf, self.sem, self.m, self.l, self.acc, self.carry, self.cfin) = scratch
        self.pt, self.oa = pt_ref, oa_ref
        self.ppc, self.nch, self.total, self.b0, self.n_heads, self.dh, self.rh = ppc, nch, total, b0, n_heads, dh, rh
        self.lf_rows = rh // LANES

    def key_copies(self, t, slot):
        out = []
        for j in range(self.ppc):
            pid = self.pt[(self.b0 * self.nch + t) * self.ppc + j]
            out.append(pltpu.make_async_copy(self.ck.at[pid], self.kbuf.at[slot, pl.ds(j * self.rh, self.rh), :],
                                             self.sem.at[0, slot]))
            out.append(pltpu.make_async_copy(self.clf.at[pid],
                                             self.lfbuf.at[slot, pl.ds(j * self.lf_rows, self.lf_rows), :],
                                             self.sem.at[2, slot]))
        return out

    def value_copies(self, t, slot):
        out = []
        for j in range(self.ppc):
            pid = self.pt[(self.b0 * self.nch + t) * self.ppc + j]
            out.append(pltpu.make_async_copy(self.cv.at[pid], self.vbuf.at[slot, pl.ds(j * self.rh, self.rh), :],
                                             self.sem.at[1, slot]))
        return out

    def _bias_and_mask(self, slot):
        n_heads = self.n_heads
        nrow = self.ppc * self.lf_rows
        lfv = self.lfbuf[slot]
        r = jnp.dot(jnp.concatenate(_split3(lfv), axis=0), self.uv[...], preferred_element_type=F32)
        r = r[0:nrow] + r[nrow:2 * nrow] + r[2 * nrow:3 * nrow]
        within, rowtot = r[:, :LANES], r[:, LANES:]
        rr = jnp.dot(self.slt[...], jnp.concatenate(_split3(rowtot), axis=1), preferred_element_type=F32)
        offs = rr[:, :LANES] + rr[:, LANES:2 * LANES] + rr[:, 2 * LANES:]
        carry = self.carry[...]
        cs = within + offs + carry
        self.carry[...] = carry + offs[nrow - 1:nrow, :] + rowtot[nrow - 1:nrow, :]
        return jnp.concatenate([jnp.broadcast_to(cs[i:i + 1, :], (n_heads, LANES)) for i in range(nrow)], axis=1)

    def _head_match(self):
        lane = lax.broadcasted_iota(jnp.int32, (self.n_heads, LANES), 1)
        return lane, jnp.bitwise_and(lane, self.n_heads - 1) == lax.broadcasted_iota(jnp.int32, (self.n_heads, LANES), 0)

    def _scores(self, t, slot):
        q = self.q[self.b0 + t // self.nch]
        s = lax.dot_general(q, self.kbuf[slot], _NT, preferred_element_type=F32)
        self.sbuf[slot] = s - self._bias_and_mask(slot)

    def _update(self, s, valid, v_rows):
        s = jnp.where(valid, s, -jnp.inf)
        m_old = self.m[...]
        m_new = jnp.maximum(m_old, jnp.max(s, axis=-1, keepdims=True))
        alpha = jnp.exp(m_old - m_new)
        pr = jnp.exp(s - m_new)
        self.l[...] = alpha * self.l[...] + jnp.sum(pr, axis=-1, keepdims=True)
        self.acc[...] = alpha * self.acc[...] + jnp.dot(pr, v_rows, preferred_element_type=F32)
        self.m[...] = m_new

    def _reset_softmax(self):
        self.m[...] = jnp.full(self.m.shape, -jnp.inf, F32)
        self.l[...] = jnp.zeros(self.l.shape, F32)
        self.acc[...] = jnp.zeros(self.acc.shape, F32)

    def _finish_batch(self, b):
        n_heads, dh = self.n_heads, self.dh
        lane, head_match = self._head_match()
        bg = self.b0 + b
        q = self.q[bg]
        zpad = jnp.zeros((LANES - n_heads, dh), F32)
        k_self = jnp.concatenate([zpad, self.kn[bg]], axis=0)
        v_self = jnp.concatenate([zpad, self.vn[bg]], axis=0)
        s_self = lax.dot_general(q, k_self, _NT, preferred_element_type=F32)
        c_t = self.cfin[...] + self.lfn[bg]
        self._update(s_self - c_t, jnp.logical_and(lane >= LANES - n_heads, head_match), v_self)
        o = self.acc[...] / self.l[...]
        ms = jnp.sum(jnp.sum(o * o, axis=-1, keepdims=True), axis=0, keepdims=True) / (n_heads * dh)
        self.oa[b] = o * lax.rsqrt(ms + EPS) * self.g[...]

    def start_first(self):
        for t in range(min(2, self.total)):
            for cp in self.key_copies(t, t) + self.value_copies(t, t):
                cp.start()
        for cp in self.key_copies(0, 0):
            cp.wait()
        self.carry[...] = jnp.zeros(self.carry.shape, F32)
        self._scores(0, 0)

        @pl.when(2 < self.total)
        def _():
            for cp in self.key_copies(2, 0):
                cp.start()

    def call(self, u, with_keys):
        nch = self.nch
        tv = u - 1
        bv = tv // nch
        cv = tv - bv * nch
        vslot = lax.rem(tv, 2)
        kslot = 1 - vslot
        if with_keys:
            for cp in self.key_copies(u, kslot):
                cp.wait()
        for cp in self.value_copies(tv, vslot):
            cp.wait()

        first_of_batch = cv == nch - 1 if with_keys else True

        @pl.when(first_of_batch)
        def _():
            self.cfin[...] = self.carry[...]
            self.carry[...] = jnp.zeros(self.carry.shape, F32)

        @pl.when(cv == 0)
        def _():
            self._reset_softmax()

        s_prev = self.sbuf[vslot]
        if with_keys:
            self._scores(u, kslot)
        _, head_match = self._head_match()
        valid = jnp.concatenate([head_match] * (self.ppc * self.lf_rows), axis=1)
        self._update(s_prev, valid, self.vbuf[vslot])

        if with_keys:
            @pl.when(u + 2 < self.total)
            def _():
                for cp in self.key_copies(u + 2, kslot):
                    cp.start()

        @pl.when(u + 1 < self.total)
        def _():
            for cp in self.value_copies(u + 1, vslot):
                cp.start()

        @pl.when(cv == nch - 1)
        def _():
            self._finish_batch(bv)


def _emit_decode_calls(dec, step, cps, slots):
    tail_step, tail_j = (dec.total - 1) // cps, (dec.total - 1) % cps
    for j in slots:
        u = step * cps + j + 1

        @pl.when(u < dec.total)
        def _(u=u):
            dec.call(u, True)

        if j == tail_j:
            @pl.when(step == tail_step)
            def _(u=u):
                dec.call(u, False)


def _decode_consts(nrow, n_heads):
    li = np.arange(LANES)
    same_head = (li[:, None] % n_heads) == (li[None, :] % n_heads)
    u = same_head & (li[:, None] <= li[None, :])
    ri = np.arange(nrow)
    slt = ri[None, :] < ri[:, None]
    as_bf16 = lambda a: jnp.asarray(a.astype(np.float32), BF16)
    return as_bf16(np.concatenate([u, same_head], axis=1)), as_bf16(slt)


N_DECODE_IN = 10


def _decode_plumbing(n_steps, page_table, q, k_new, v_new, lf_new, g_attn, cache_k, cache_v, cache_lf, ppc, b0, b1):
    nb, n_heads, dh = q.shape
    rh = cache_k.shape[1]
    nch = page_table.shape[1] // ppc
    total = (b1 - b0) * nch
    lf_rows = rh // LANES
    uv, slt = _decode_consts(ppc * lf_rows, n_heads)
    dcfg = dict(ppc=ppc, nch=nch, total=total, b0=b0, n_heads=n_heads, dh=dh, rh=rh)
    whole = lambda a: pl.BlockSpec(a.shape, lambda i, j, pt: (0,) * a.ndim)
    args = (q, k_new, v_new, lf_new, g_attn, uv, slt, cache_k, cache_v, cache_lf)
    in_specs = [whole(a) for a in args[:7]] + [pl.BlockSpec(memory_space=pl.ANY)] * 3
    out_spec = pl.BlockSpec((b1 - b0, n_heads, dh), lambda i, j, pt: (0, 0, 0))
    out_shape = jax.ShapeDtypeStruct((b1 - b0, n_heads, dh), F32)
    scratch = [
        pltpu.VMEM((2, ppc * rh, dh), F32), pltpu.VMEM((2, ppc * rh, dh), F32),
        pltpu.VMEM((2, ppc * lf_rows, LANES), F32),
        pltpu.VMEM((2, n_heads, ppc * rh), F32),
        pltpu.SemaphoreType.DMA((3, 2)),
        pltpu.VMEM((n_heads, 1), F32), pltpu.VMEM((n_heads, 1), F32), pltpu.VMEM((n_heads, dh), F32),
        pltpu.VMEM((1, LANES), F32), pltpu.VMEM((1, LANES), F32),
    ]
    return dict(cps=-(-total // n_steps), dcfg=dcfg), args, in_specs, out_spec, out_shape, scratch


def _ffn_decode_body(pt_ref, hf_ref, x1_ref, wg_ref, wu_ref, wd_ref, g_ref, *refs, cps, dcfg):
    dec_in, (o_ref, oa_ref), scratch = refs[:N_DECODE_IN], refs[N_DECODE_IN:N_DECODE_IN + 2], refs[N_DECODE_IN + 2:]
    act_ref, scratch = scratch[0], scratch[1:]
    dec = _Decode(pt_ref, dec_in, oa_ref, scratch, **dcfg)
    f = pl.program_id(1)
    nf = pl.num_programs(1)
    step = pl.program_id(0) * nf + f
    tf = wg_ref.shape[1]
    n_sub = tf // FFN_SUB

    @pl.when(step == 0)
    def _():
        dec.start_first()

    @pl.when(f == 0)
    def _():
        o_ref[...] = x1_ref[...]

    def gate_up(j):
        sl = slice(j * FFN_SUB, (j + 1) * FFN_SUB)
        hf = hf_ref[...]
        gate = jnp.dot(hf, wg_ref[:, sl], preferred_element_type=F32)
        up = jnp.dot(hf, wu_ref[:, sl], preferred_element_type=F32)
        act_ref[:, sl] = (_silu(gate) * up).astype(BF16)

    def down():
        d = o_ref.shape[1]
        for n0 in range(0, d, 512):
            o_ref[:, n0:n0 + 512] += jnp.dot(act_ref[...], wd_ref[:, n0:n0 + 512], preferred_element_type=F32)

    pieces = [functools.partial(gate_up, j) for j in range(n_sub)] + [down]
    per_piece = [cps // len(pieces) + (1 if i < cps % len(pieces) else 0) for i in range(len(pieces))]
    j = 0
    for piece, n_calls in zip(pieces, per_piece):
        _emit_decode_calls(dec, step, cps, range(j, j + n_calls))
        j += n_calls
        piece()

    @pl.when(f == nf - 1)
    def _():
        y = o_ref[...]
        o_ref[...] = y * lax.rsqrt(jnp.mean(y * y, axis=-1, keepdims=True) + EPS) * g_ref[...]


def _ffn_decode(hf, x1, wg, wu, wd, g_final, tm, tf, decode_operands, b0, b1):
    m, d = x1.shape
    dff = wg.shape[1]
    n_steps = (m // tm) * (dff // tf)
    static, dec_args, dec_specs, oa_spec, oa_shape, dec_scratch = _decode_plumbing(n_steps, *decode_operands, b0, b1)
    grid_spec = pltpu.PrefetchScalarGridSpec(
        num_scalar_prefetch=1,
        grid=(m // tm, dff // tf),
        in_specs=[pl.BlockSpec((tm, d), lambda i, f, pt: (i, 0), pipeline_mode=pl.Buffered(1)),
                  pl.BlockSpec((tm, d), lambda i, f, pt: (i, 0), pipeline_mode=pl.Buffered(1)),
                  pl.BlockSpec((d, tf), lambda i, f, pt: (0, f)), pl.BlockSpec((d, tf), lambda i, f, pt: (0, f)),
                  pl.BlockSpec((tf, d), lambda i, f, pt: (f, 0)),
                  pl.BlockSpec(g_final.shape, lambda i, f, pt: (0, 0))] + dec_specs,
        out_specs=[pl.BlockSpec((tm, d), lambda i, f, pt: (i, 0), pipeline_mode=pl.Buffered(1)), oa_spec],
        scratch_shapes=[pltpu.VMEM((tm, tf), BF16)] + dec_scratch,
    )
    return pl.pallas_call(
        functools.partial(_ffn_decode_body, **static),
        grid_spec=grid_spec,
        out_shape=[jax.ShapeDtypeStruct((m, d), F32), oa_shape],
        compiler_params=_params(("arbitrary", "arbitrary")),
        name="ffn_with_paged_decode",
    )(decode_operands[0].reshape(-1), hf, x1, wg, wu, wd, g_final, *dec_args)


def _ssd_step_body(xbc_ref, sc_ref, z_ref, dt_ref, st_ref, cw_ref, cb_ref, arow_ref, e_ref, dsk_ref, g_ref,
                   o_ref, sto_ref, sco_ref, *, ds, n_groups, n_state, n_heads, p, hoff, kw):
    xn = xbc_ref[...]
    a = cw_ref[0:1, :] * sc_ref[0:1, :]
    for i in range(1, kw - 1):
        a = a + cw_ref[i:i + 1, :] * sc_ref[i:i + 1, :]
    a = cb_ref[...] + (a + cw_ref[kw - 1:kw, :] * xn)
    act = _silu(a)
    for i in range(kw - 2):
        sco_ref[i:i + 1, :] = sc_ref[i + 1:i + 2, :]
    sco_ref[kw - 2:kw - 1, :] = xn

    dt = dt_ref[...]
    da = dt * arow_ref[...]
    dt_exp = _dot_r01(jnp.broadcast_to(dt, (SUBLANES, LANES)), e_ref[...])[0:1, :]
    decay = jnp.exp(da)
    xs = act[:, :ds]
    xdt = xs * dt_exp
    hpg = n_heads // n_groups
    gw = hpg * p
    eye = lax.broadcasted_iota(jnp.int32, (gw, gw), 0) == lax.broadcasted_iota(jnp.int32, (gw, gw), 1)
    y_parts = []
    for g in range(n_groups):
        bg = act[:, ds + g * n_state:ds + (g + 1) * n_state]
        cg = act[:, ds + (n_groups + g) * n_state:ds + (n_groups + g + 1) * n_state]
        diag = jnp.where(eye, jnp.broadcast_to(xdt[:, g * gw:(g + 1) * gw], (gw, gw)), 0.0).astype(BF16)
        outer = jnp.dot(diag, jnp.broadcast_to(bg, (gw, n_state)).astype(BF16), preferred_element_type=F32)
        for jj in range(hpg):
            j = g * hpg + jj
            rows = slice(j * p, (j + 1) * p)
            sto_ref[rows, :] = st_ref[rows, :] * decay[:, hoff + j:hoff + j + 1] + outer[jj * p:(jj + 1) * p, :]
        st_new = sto_ref[g * gw:(g + 1) * gw, :].astype(BF16)
        c8 = jnp.broadcast_to(cg, (SUBLANES, n_state)).astype(BF16)
        y_parts.append(lax.dot_general(c8, st_new, _NT, preferred_element_type=F32)[0:1, :])
    y = jnp.concatenate(y_parts, axis=1)
    y = (y + dsk_ref[...] * xs) * _silu(z_ref[...])
    o_ref[...] = (y * lax.rsqrt(jnp.mean(y * y, axis=-1, keepdims=True) + EPS) * g_ref[...]).astype(o_ref.dtype)


def _ssd_step(xbc, state_conv, z, dt, state, conv_w, conv_b, a_row, expand, dskip_row, g_ssd, dims):
    ds, n_groups, n_state, n_heads, p, hoff, kw = dims
    nb, _, dc = xbc.shape
    per_b = lambda b: (b, 0, 0)
    body = functools.partial(_ssd_step_body, ds=ds, n_groups=n_groups, n_state=n_state, n_heads=n_heads,
                             p=p, hoff=hoff, kw=kw)
    return pl.pallas_call(
        body,
        grid=(nb,),
        in_specs=[
            pl.BlockSpec((None, 1, dc), per_b), pl.BlockSpec((None, kw - 1, dc), per_b),
            pl.BlockSpec((None, 1, ds), per_b), pl.BlockSpec((None, 1, LANES), per_b),
            pl.BlockSpec((None, n_heads * p, n_state), per_b),
            _resident((kw, dc)), _resident((1, dc)), _resident((1, LANES)), _resident(expand.shape),
            _resident((1, ds)), _resident((1, ds)),
        ],
        out_specs=[pl.BlockSpec((None, 1, ds), per_b), pl.BlockSpec((None, n_heads * p, n_state), per_b),
                   pl.BlockSpec((None, kw - 1, dc), per_b)],
        out_shape=[jax.ShapeDtypeStruct((nb, 1, ds), BF16), jax.ShapeDtypeStruct(state.shape, F32),
                   jax.ShapeDtypeStruct(state_conv.shape, F32)],
        compiler_params=_params(("arbitrary",)),
        name="ssd_step",
    )(xbc, state_conv, z, dt, state, conv_w, conv_b, a_row, expand, dskip_row, g_ssd)


def _pick_tile(m, pref):
    t = min(m, pref)
    while m % t:
        t //= 2
    return t


def kernel(x_prompt, x_sample, cache_k, cache_v, cache_logf, state_ssm, state_conv, page_table, g_mix, w_in, b_fgate, conv_w, conv_b, dt_bias, a_log, d_skip, g_attn, g_ssd, w_out, g_ffn, w_gate, w_up, w_down, g_final):
    bp, seq, d = x_prompt.shape
    bs = x_sample.shape[0]
    depth, n_pool, page, n_heads, dh = cache_k.shape
    da = n_heads * dh
    nh, p, n_state = state_ssm.shape[2:]
    ds = nh * p
    kw = conv_w.shape[1]
    dc = conv_w.shape[2]
    n_groups = (dc - ds) // (2 * n_state)
    hoff = n_heads
    assert bs >= 2 and depth == 1, "the FFN kernel applies the final rmsnorm, so it must be the last layer"
    assert x_sample.shape[1] == 1 and seq % SSD_CHUNK == 0 and n_heads + nh <= LANES - n_heads
    assert n_heads & (n_heads - 1) == 0
    assert (page * n_heads) % LANES == 0 and p * 2 == LANES and (nh // n_groups) % 2 == 0
    dims = (ds, n_groups, n_state, nh, p, hoff, kw)
    o_f, o_z, o_x, o_dt = 3 * da, 3 * da + n_heads, 3 * da + n_heads + ds, 3 * da + n_heads + ds + dc
    mp = bp * seq

    expand_np = np.zeros((LANES, ds), np.float32)
    for j in range(nh):
        expand_np[hoff + j, j * p:(j + 1) * p] = 1.0
    expand = jnp.asarray(expand_np, BF16)

    yp = x_prompt.reshape(mp, d)
    ys = x_sample.reshape(bs, d)
    outs = [[] for _ in range(10)]
    for l in range(depth):
        w = w_in[l]
        w_q, w_k, w_v = (w[:, i * da:(i + 1) * da].astype(BF16) for i in range(3))
        w_f = w[:, o_f:o_f + n_heads]
        w_fd = jnp.concatenate([w_f, w[:, o_dt:o_dt + nh], jnp.zeros((d, LANES - 2 * n_heads - nh), w.dtype), w_f],
                               axis=1).astype(BF16)
        w_z = w[:, o_z:o_z + ds].astype(BF16)
        w_x = w[:, o_x:o_x + dc].astype(BF16)
        w_oa = w_out[l][:da].astype(BF16)
        w_os = w_out[l][da:].astype(BF16)
        wg, wu, wd = w_gate[l].astype(BF16), w_up[l].astype(BF16), w_down[l].astype(BF16)
        zpad = jnp.zeros((LANES - 2 * n_heads - nh,), F32)
        bias_row = jnp.concatenate([b_fgate[l], dt_bias[l], zpad, b_fgate[l]]).reshape(1, LANES)
        a_row = jnp.concatenate([jnp.zeros((n_heads,), F32), -jnp.exp(a_log[l].astype(F32)),
                                 jnp.zeros((LANES - n_heads - nh,), F32)]).reshape(1, LANES)
        dskip_row = jnp.repeat(d_skip[l].astype(F32), p).reshape(1, ds)
        gm, ga, gs, gf = (g_mix[l].reshape(1, d), g_attn[l].reshape(1, da), g_ssd[l].reshape(1, ds),
                          g_ffn[l].reshape(1, d))
        gfin = g_final.reshape(1, d)
        cw, cb = conv_w[l].astype(F32), conv_b[l].reshape(1, dc).astype(F32)
        proj_a = dict(weights=[w_q, w_k, w_v, w_fd], out_map=[[0], [1, 3], [2, 4], [5]],
                      out_dtypes=[BF16, F32, F32, BF16, BF16, F32])
        proj_b = dict(weights=[w_z, w_x], out_map=[[0], [1]], out_dtypes=[F32, F32])

        tm = _pick_tile(mp, 512)
        q, k, v, kb, vb, fd = _norm_proj(yp, gm, tm=tm, name="in_proj_qkv", out_scale={0: dh ** -0.5 * LOG2E}, **proj_a)
        z, xbc = _norm_proj(yp, gm, tm=_pick_tile(mp, 256), name="in_proj_ssd", **proj_b)
        qs, k_s, v_s, _, _, fds = _norm_proj(ys, gm, tm=bs, name="in_proj_qkv_s", out_scale={0: dh ** -0.5}, **proj_a)
        zs, xbcs = _norm_proj(ys, gm, tm=bs, name="in_proj_ssd_s", **proj_b)
        lf, dt, cs, cst = _gates(fd, bias_row, a_row, bp, seq, n_heads)
        lfs, dts = _sgates(fds, bias_row)
        decode_operands = (page_table, qs.astype(F32).reshape(bs, n_heads, dh), k_s.reshape(bs, n_heads, dh),
                           v_s.reshape(bs, n_heads, dh), lfs.reshape(bs, 1, LANES), ga.reshape(n_heads, dh),
                           cache_k[l].reshape(n_pool, page * n_heads, dh), cache_v[l].reshape(n_pool, page * n_heads, dh),
                           cache_logf[l].astype(F32).reshape(n_pool, page * n_heads // LANES, LANES),
                           _pick_tile(page_table.shape[1], DECODE_PAGES_PER_CHUNK))
        b_split = max(1, bs // DECODE_ATTENTION_SHARE)

        oa, oas_a = _attention(q, kb, vb, cst, ga, bp, seq, n_heads, dh, _pick_tile(seq, 256), decode_operands, 0, b_split)
        osm, st_p = _ssd_prompt(xbc, z, dt, cs, cst, cw, cb, dskip_row, gs, bp, seq, dims)
        x1, hf = _outproj(oa, osm, yp, w_oa, w_os, gf, tm, "out_proj")
        tf = _pick_tile(wg.shape[1], 512)
        yp_new, oas_f = _ffn_decode(hf, x1, wg, wu, wd, gfin, _pick_tile(mp, 1024), tf, decode_operands, b_split, bs)
        oas = jnp.concatenate([oas_a, oas_f], axis=0)

        oss, st_s, conv_s = _ssd_step(xbcs.reshape(bs, 1, dc), state_conv[l].astype(F32), zs.reshape(bs, 1, ds),
                                      dts.reshape(bs, 1, LANES), state_ssm[l].astype(F32).reshape(bs, nh * p, n_state),
                                      cw, cb, a_row, expand, dskip_row, gs, dims)
        x1s, hfs = _outproj(oas.astype(BF16).reshape(bs, da), oss.reshape(bs, ds), ys, w_oa, w_os, gf, bs, "out_proj_s")
        ys_new = _ffn(hfs, x1s, wg, wu, wd, gfin, bs, tf, "ffn_s")

        yp, ys = yp_new, ys_new
        new = [k.reshape(bp, seq, n_heads, dh), v.reshape(bp, seq, n_heads, dh),
               lf[:, :n_heads].reshape(bp, seq, n_heads),
               k_s.reshape(bs, 1, n_heads, dh), v_s.reshape(bs, 1, n_heads, dh),
               lfs[:, :n_heads].reshape(bs, 1, n_heads),
               st_p.reshape(bp, nh, p, n_state), xbc.reshape(bp, seq, dc)[:, seq - kw + 1:],
               st_s.reshape(bs, nh, p, n_state), conv_s]
        for acc, val in zip(outs, new):
            acc.append(val)

    return (yp.reshape(bp, seq, d), ys.reshape(bs, 1, d)) + tuple(jnp.stack(o) for o in outs)
```

```python
import functools

import numpy as np
import jax
import jax.numpy as jnp
from jax import lax
from jax.experimental import pallas as pl
from jax.experimental.pallas import tpu as pltpu

F32 = jnp.float32
BF16 = jnp.bfloat16
EPS = 1e-6
LOG2E = 1.4426950408889634
LANES = 128
SUBLANES = 8
SSD_CHUNK = 128
VMEM_LIMIT_MB = 56
FFN_SUB = 256
DECODE_PAGES_PER_CHUNK = 8
DECODE_ATTENTION_SHARE = 8

_NT = (((1,), (1,)), ((), ()))
_TN = (((0,), (0,)), ((), ()))


def _params(sem):
    return pltpu.CompilerParams(dimension_semantics=sem, vmem_limit_bytes=VMEM_LIMIT_MB * 2**20)


def _resident(shape):
    nd = len(shape)
    return pl.BlockSpec(shape, lambda *_: (0,) * nd, pipeline_mode=pl.Buffered(1))


def _silu(x):
    return x / (1.0 + jnp.exp(-x))


def _split3(x):
    hi = x.astype(BF16)
    r1 = x - hi.astype(F32)
    mid = r1.astype(BF16)
    lo = (r1 - mid.astype(F32)).astype(BF16)
    return hi, mid, lo


def _dot_r01(a, b01):
    hi, mid, lo = _split3(a)
    d = functools.partial(jnp.dot, preferred_element_type=F32)
    return d(hi, b01) + d(mid, b01) + d(lo, b01)


def _dot_l01(a01, b):
    hi, mid, lo = _split3(b)
    d = functools.partial(jnp.dot, preferred_element_type=F32)
    return d(a01, hi) + d(a01, mid) + d(a01, lo)


def _norm_proj_body(x_ref, g_ref, *refs, n_w, out_map, out_scale, tn):
    w_refs, o_refs = refs[:n_w], refs[n_w:]
    x = x_ref[...]
    h = (x * lax.rsqrt(jnp.mean(x * x, axis=-1, keepdims=True) + EPS) * g_ref[...]).astype(BF16)
    for wi, w_ref in enumerate(w_refs):
        n = w_ref.shape[1]
        for n0 in range(0, n, tn):
            n1 = min(n0 + tn, n)
            r = jnp.dot(h, w_ref[:, n0:n1], preferred_element_type=F32)
            for oi in out_map[wi]:
                o_refs[oi][:, n0:n1] = (r * out_scale[oi] if oi in out_scale else r).astype(o_refs[oi].dtype)


def _norm_proj(x, g, weights, out_map, out_dtypes, tm, name, out_scale=None):
    m, d = x.shape
    widths = {}
    for wi, ois in enumerate(out_map):
        for oi in ois:
            widths[oi] = weights[wi].shape[1]
    n_out = len(out_dtypes)
    body = functools.partial(_norm_proj_body, n_w=len(weights), out_map=out_map, out_scale=out_scale or {}, tn=512)
    return pl.pallas_call(
        body,
        grid=(m // tm,),
        in_specs=[pl.BlockSpec((tm, d), lambda i: (i, 0)), _resident((1, d))]
        + [_resident(w.shape) for w in weights],
        out_specs=[pl.BlockSpec((tm, widths[oi]), lambda i: (i, 0)) for oi in range(n_out)],
        out_shape=[jax.ShapeDtypeStruct((m, widths[oi]), out_dtypes[oi]) for oi in range(n_out)],
        compiler_params=_params(("arbitrary",)),
        name=name,
    )(x, g, *weights)


def _gates_body(fd_ref, bias_ref, arow_ref, lf_ref, dt_ref, cs_ref, cst_ref, *, n_heads):
    t = fd_ref.shape[0]
    L = SSD_CHUNK
    lane = lax.broadcasted_iota(jnp.int32, (L, LANES), 1)
    tri = (lax.broadcasted_iota(jnp.int32, (L, L), 0) >= lax.broadcasted_iota(jnp.int32, (L, L), 1)).astype(BF16)
    is_f = lane < n_heads
    carry = jnp.zeros((1, LANES), F32)
    for c in range(t // L):
        rows = slice(c * L, (c + 1) * L)
        v = fd_ref[rows, :] + bias_ref[...]
        sp = jnp.log1p(jnp.exp(-jnp.abs(v)))
        lf = jnp.minimum(v, 0.0) - sp
        dt = jnp.maximum(v, 0.0) + sp
        zc = jnp.where(is_f, lf, dt * arow_ref[...])
        cs = _dot_l01(tri, zc) + jnp.where(is_f, carry, 0.0)
        carry = cs[L - 1:L, :]
        lf_ref[rows, :] = lf
        dt_ref[rows, :] = dt
        cs_ref[rows, :] = cs
        cst_ref[:, rows] = cs.T


def _gates(fd, bias_row, a_row, batch, seq, n_heads):
    m = fd.shape[0]
    row = lambda b: (b, 0)
    return pl.pallas_call(
        functools.partial(_gates_body, n_heads=n_heads),
        grid=(batch,),
        in_specs=[pl.BlockSpec((seq, LANES), row), _resident((1, LANES)), _resident((1, LANES))],
        out_specs=[pl.BlockSpec((seq, LANES), row)] * 3 + [pl.BlockSpec((None, LANES, seq), lambda b: (b, 0, 0))],
        out_shape=[jax.ShapeDtypeStruct((m, LANES), F32)] * 3 + [jax.ShapeDtypeStruct((batch, LANES, seq), F32)],
        compiler_params=_params(("arbitrary",)),
        name="gates",
    )(fd, bias_row, a_row)


def _sgates_body(fd_ref, bias_ref, lf_ref, dt_ref):
    v = fd_ref[...] + bias_ref[...]
    sp = jnp.log1p(jnp.exp(-jnp.abs(v)))
    lf_ref[...] = jnp.minimum(v, 0.0) - sp
    dt_ref[...] = jnp.maximum(v, 0.0) + sp


def _sgates(fd, bias_row):
    return pl.pallas_call(
        _sgates_body,
        out_shape=[jax.ShapeDtypeStruct(fd.shape, F32)] * 2,
        name="sample_gates",
    )(fd, bias_row)


def _attn_body(pt_ref, q_ref, k_ref, v_ref, ct_ref, g_ref, *refs, tq, n_heads, dh, cps, dcfg):
    dec_in, (o_ref, oa_ref), scratch = refs[:N_DECODE_IN], refs[N_DECODE_IN:N_DECODE_IN + 2], refs[N_DECODE_IN + 2:]
    (m_ref, l_ref, acc_ref), dec_scratch = scratch[:3], scratch[3:]
    qi = pl.program_id(1)

    dec = _Decode(pt_ref, dec_in, oa_ref, dec_scratch, **dcfg)
    step = pl.program_id(0) * pl.num_programs(1) + qi

    @pl.when(step == 0)
    def _():
        dec.start_first()

    _emit_decode_calls(dec, step, cps, range(0, (cps + 1) // 2))

    causal = lax.broadcasted_iota(jnp.int32, (tq, tq), 0) >= lax.broadcasted_iota(jnp.int32, (tq, tq), 1)
    m_ref[...] = jnp.full(m_ref.shape, -jnp.inf, F32)
    l_ref[...] = jnp.zeros(l_ref.shape, F32)
    acc_ref[...] = jnp.zeros(acc_ref.shape, F32)

    def kv_block(kj, masked):
        k0 = pl.multiple_of(kj * tq, tq)
        for h in range(n_heads):
            sl = slice(h * dh, (h + 1) * dh)
            s = lax.dot_general(q_ref[:, sl], k_ref[pl.ds(k0, tq), sl], _NT, preferred_element_type=F32)
            s = s - ct_ref[h:h + 1, pl.ds(k0, tq)] * LOG2E
            if masked:
                s = jnp.where(causal, s, -jnp.inf)
            m_old = m_ref[h]
            m_new = jnp.maximum(m_old, jnp.max(s, axis=-1, keepdims=True))
            alpha = jnp.exp2(m_old - m_new)
            p = jnp.exp2(s - jnp.concatenate([m_new] * (tq // LANES), axis=1))
            psum = p[:, :LANES]
            for c0 in range(LANES, tq, LANES):
                psum = psum + p[:, c0:c0 + LANES]
            l_ref[h] = alpha * l_ref[h] + psum
            acc_ref[:, sl] = alpha * acc_ref[:, sl] + jnp.dot(p.astype(BF16), v_ref[pl.ds(k0, tq), sl],
                                                              preferred_element_type=F32)
            m_ref[h] = m_new

    def loop_body(kj, carry):
        kv_block(kj, False)
        return carry

    lax.fori_loop(0, qi, loop_body, 0)
    _emit_decode_calls(dec, step, cps, range((cps + 1) // 2, cps))
    kv_block(qi, True)
    for h in range(n_heads):
        sl = slice(h * dh, (h + 1) * dh)
        acc_ref[:, sl] = acc_ref[:, sl] / jnp.sum(l_ref[h], axis=-1, keepdims=True)
    o = acc_ref[...]
    o_ref[...] = (o * lax.rsqrt(jnp.mean(o * o, axis=-1, keepdims=True) + EPS) * g_ref[...]).astype(o_ref.dtype)


def _attention(q, kb, vb, cst, g_attn, batch, seq, n_heads, dh, tq, decode_operands, b0, b1):
    m, da = q.shape
    nq = seq // tq
    static, dec_args, dec_specs, oa_spec, oa_shape, dec_scratch = _decode_plumbing(batch * nq, *decode_operands, b0, b1)
    body = functools.partial(_attn_body, tq=tq, n_heads=n_heads, dh=dh, **static)
    grid_spec = pltpu.PrefetchScalarGridSpec(
        num_scalar_prefetch=1,
        grid=(batch, nq),
        in_specs=[
            pl.BlockSpec((tq, da), lambda b, i, pt: (b * nq + i, 0)),
            pl.BlockSpec((seq, da), lambda b, i, pt: (b, 0)),
            pl.BlockSpec((seq, da), lambda b, i, pt: (b, 0)),
            pl.BlockSpec((None, SUBLANES, seq), lambda b, i, pt: (b, 0, 0)),
            pl.BlockSpec((1, da), lambda b, i, pt: (0, 0)),
        ] + dec_specs,
        out_specs=[pl.BlockSpec((tq, da), lambda b, i, pt: (b * nq + i, 0)), oa_spec],
        scratch_shapes=[pltpu.VMEM((n_heads, tq, LANES), F32), pltpu.VMEM((n_heads, tq, LANES), F32),
                        pltpu.VMEM((tq, da), F32)] + dec_scratch,
    )
    return pl.pallas_call(
        body,
        grid_spec=grid_spec,
        out_shape=[jax.ShapeDtypeStruct((m, da), BF16), oa_shape],
        compiler_params=_params(("arbitrary", "arbitrary")),
        name="fox_attention_with_paged_decode",
    )(decode_operands[0].reshape(-1), q, kb, vb, cst, g_attn, *dec_args)


def _ssd_body(xbc_ref, z_ref, dt_ref, cs_ref, cst_ref, cw_ref, cb_ref, dsk_ref, g_ref,
              o_ref, st_ref, xp_ref, act_ref, y_ref, *, ds, n_groups, n_state, n_heads, p, hoff, kw):
    L = SSD_CHUNK
    dc = xbc_ref.shape[1]
    c = pl.program_id(1)
    pad = SUBLANES

    @pl.when(c == 0)
    def _():
        st_ref[...] = jnp.zeros_like(st_ref)
        xp_ref[:, 0:pad, :] = jnp.zeros((dc // LANES, pad, LANES), F32)

    slab = 512
    for t in range(dc // LANES):
        sl = slice(t * LANES, (t + 1) * LANES)
        xp_ref[t, pad:pad + L, :] = xbc_ref[:, sl]
        a = cw_ref[0:1, sl] * xp_ref[t, pad - kw + 1:pad - kw + 1 + L, :]
        for i in range(1, kw):
            a = a + cw_ref[i:i + 1, sl] * xp_ref[t, pad - kw + 1 + i:pad - kw + 1 + i + L, :]
        a = cb_ref[:, sl] + a
        act_ref[:, sl] = _silu(a)
        xp_ref[t, 0:pad, :] = xp_ref[t, L:L + pad, :]

    dtv = dt_ref[...]
    cumv = cs_ref[...]
    cumt = cst_ref[...]
    lane = lax.broadcasted_iota(jnp.int32, (L, LANES), 1)
    causal = lax.broadcasted_iota(jnp.int32, (L, L), 0) >= lax.broadcasted_iota(jnp.int32, (L, L), 1)
    first = lane < p
    hpg = n_heads // n_groups
    gw = hpg * p
    for g in range(n_groups):
        bg = act_ref[:, ds + g * n_state:ds + (g + 1) * n_state].astype(BF16)
        cg = act_ref[:, ds + (n_groups + g) * n_state:ds + (n_groups + g + 1) * n_state].astype(BF16)
        cbm = lax.dot_general(cg, bg, _NT, preferred_element_type=F32)
        st_g = st_ref[g * gw:(g + 1) * gw, :]
        yoff = lax.dot_general(cg, st_g.astype(BF16), _NT, preferred_element_type=F32)
        xd_parts = []
        for pr in range(hpg // 2):
            j0 = g * hpg + 2 * pr
            la, lb = hoff + j0, hoff + j0 + 1
            c0v, c1v = cumv[:, la:la + 1], cumv[:, lb:lb + 1]
            d0 = jnp.exp(jnp.where(causal, c0v - cumt[la:la + 1, :], -jnp.inf))
            d1 = jnp.exp(jnp.where(causal, c1v - cumt[lb:lb + 1, :], -jnp.inf))
            mm = jnp.concatenate([(cbm * d0).astype(BF16), (cbm * d1).astype(BF16)], axis=1)
            cols = slice(j0 * p, (j0 + 2) * p)
            xdt = act_ref[:, cols] * jnp.where(first, dtv[:, la:la + 1], dtv[:, lb:lb + 1])
            bd = jnp.concatenate([jnp.where(first, xdt, 0.0).astype(BF16),
                                  jnp.where(first, 0.0, xdt).astype(BF16)], axis=0)
            ydiag = jnp.dot(mm, bd, preferred_element_type=F32)
            cum_pair = jnp.where(first, c0v, c1v)
            y_ref[:, cols] = ydiag + yoff[:, 2 * pr * p:(2 * pr + 2) * p] * jnp.exp(cum_pair)
            last_pair = jnp.where(first[0:1, :], cumv[L - 1:L, la:la + 1], cumv[L - 1:L, lb:lb + 1])
            xd_parts.append((xdt * jnp.exp(last_pair - cum_pair)).astype(BF16))
        xd = jnp.concatenate(xd_parts, axis=1)
        snew = lax.dot_general(xd, bg, _TN, preferred_element_type=F32)
        for jj in range(hpg):
            j = g * hpg + jj
            cd = jnp.exp(cumv[L - 1:L, hoff + j:hoff + j + 1])
            rows = slice(j * p, (j + 1) * p)
            st_ref[rows, :] = st_ref[rows, :] * cd + snew[jj * p:(jj + 1) * p, :]

    ssq = jnp.zeros((L, 1), F32)
    for c0 in range(0, ds, slab):
        sl = slice(c0, c0 + slab)
        y = (y_ref[:, sl] + dsk_ref[:, sl] * act_ref[:, sl]) * _silu(z_ref[:, sl])
        y_ref[:, sl] = y
        ssq = ssq + jnp.sum(y * y, axis=-1, keepdims=True)
    rinv = lax.rsqrt(ssq / ds + EPS)
    for c0 in range(0, ds, slab):
        sl = slice(c0, c0 + slab)
        o_ref[:, sl] = (y_ref[:, sl] * rinv * g_ref[:, sl]).astype(o_ref.dtype)


def _ssd_prompt(xbc, z, dt, cs, cst, conv_w, conv_b, dskip_row, g_ssd, batch, seq, dims):
    ds, n_groups, n_state, n_heads, p, hoff, kw = dims
    m, dc = xbc.shape
    L = SSD_CHUNK
    nc = seq // L
    row = lambda b, c: (b * nc + c, 0)
    body = functools.partial(_ssd_body, ds=ds, n_groups=n_groups, n_state=n_state, n_heads=n_heads,
                             p=p, hoff=hoff, kw=kw)
    return pl.pallas_call(
        body,
        grid=(batch, nc),
        in_specs=[
            pl.BlockSpec((L, dc), row),
            pl.BlockSpec((L, ds), row),
            pl.BlockSpec((L, LANES), row),
            pl.BlockSpec((L, LANES), row),
            pl.BlockSpec((None, LANES, L), lambda b, c: (b, 0, c)),
            _resident((kw, dc)), _resident((1, dc)), _resident((1, ds)), _resident((1, ds)),
        ],
        out_specs=[pl.BlockSpec((L, ds), row),
                   pl.BlockSpec((None, n_heads * p, n_state), lambda b, c: (b, 0, 0))],
        out_shape=[jax.ShapeDtypeStruct((m, ds), BF16),
                   jax.ShapeDtypeStruct((batch, n_heads * p, n_state), F32)],
        scratch_shapes=[pltpu.VMEM((dc // LANES, L + SUBLANES, LANES), F32), pltpu.VMEM((L, dc), F32),
                        pltpu.VMEM((L, ds), F32)],
        compiler_params=_params(("arbitrary", "arbitrary")),
        name="ssd_prompt",
    )(xbc, z, dt, cs, cst, conv_w, conv_b, dskip_row, g_ssd)


def _outproj_body(oa_ref, os_ref, x_ref, wa_ref, ws_ref, g_ref, x1_ref, hf_ref, *, tn):
    d = x_ref.shape[1]
    oa, osm = oa_ref[...], os_ref[...]
    for n0 in range(0, d, tn):
        sl = slice(n0, n0 + tn)
        mix = jnp.dot(oa, wa_ref[:, sl], preferred_element_type=F32) + jnp.dot(osm, ws_ref[:, sl], preferred_element_type=F32)
        x1_ref[:, sl] = x_ref[:, sl] + mix
    x1 = x1_ref[...]
    hf_ref[...] = (x1 * lax.rsqrt(jnp.mean(x1 * x1, axis=-1, keepdims=True) + EPS) * g_ref[...]).astype(hf_ref.dtype)


def _outproj(oa, osm, x, wa, ws, g_ffn, tm, name):
    m, d = x.shape
    row = lambda i: (i, 0)
    return pl.pallas_call(
        functools.partial(_outproj_body, tn=512),
        grid=(m // tm,),
        in_specs=[pl.BlockSpec((tm, oa.shape[1]), row), pl.BlockSpec((tm, osm.shape[1]), row),
                  pl.BlockSpec((tm, d), row), _resident(wa.shape), _resident(ws.shape), _resident((1, d))],
        out_specs=[pl.BlockSpec((tm, d), row)] * 2,
        out_shape=[jax.ShapeDtypeStruct((m, d), F32), jax.ShapeDtypeStruct((m, d), BF16)],
        compiler_params=_params(("arbitrary",)),
        name=name,
    )(oa, osm, x, wa, ws, g_ffn)


def _ffn_body(hf_ref, x1_ref, wg_ref, wu_ref, wd_ref, g_ref, o_ref):
    f = pl.program_id(1)

    @pl.when(f == 0)
    def _():
        o_ref[...] = x1_ref[...]

    hf = hf_ref[...]
    tf = wg_ref.shape[1]
    acts = []
    for c0 in range(0, tf, FFN_SUB):
        sl = slice(c0, min(c0 + FFN_SUB, tf))
        gate = jnp.dot(hf, wg_ref[:, sl], preferred_element_type=F32)
        up = jnp.dot(hf, wu_ref[:, sl], preferred_element_type=F32)
        acts.append((_silu(gate) * up).astype(BF16))
    act = acts[0] if len(acts) == 1 else jnp.concatenate(acts, axis=1)
    part = jnp.dot(act, wd_ref[...], preferred_element_type=F32)

    o_ref[...] += part

    @pl.when(f == pl.num_programs(1) - 1)
    def _():
        y = o_ref[...]
        o_ref[...] = y * lax.rsqrt(jnp.mean(y * y, axis=-1, keepdims=True) + EPS) * g_ref[...]


def _ffn(hf, x1, wg, wu, wd, g_final, tm, tf, name):
    m, d = x1.shape
    dff = wg.shape[1]
    return pl.pallas_call(
        _ffn_body,
        grid=(m // tm, dff // tf),
        in_specs=[pl.BlockSpec((tm, d), lambda i, f: (i, 0)), pl.BlockSpec((tm, d), lambda i, f: (i, 0)),
                  pl.BlockSpec((d, tf), lambda i, f: (0, f)), pl.BlockSpec((d, tf), lambda i, f: (0, f)),
                  pl.BlockSpec((tf, d), lambda i, f: (f, 0)), _resident((1, d))],
        out_specs=pl.BlockSpec((tm, d), lambda i, f: (i, 0)),
        out_shape=jax.ShapeDtypeStruct((m, d), F32),
        compiler_params=_params(("arbitrary", "arbitrary")),
        name=name,
    )(hf, x1, wg, wu, wd, g_final)


class _Decode:
    def __init__(self, pt_ref, in_refs, oa_ref, scratch, *, ppc, nch, total, b0, n_heads, dh, rh):
        (self.q, self.kn, self.vn, self.lfn, self.g, self.uv, self.slt, self.ck, self.cv, self.clf) = in_refs
        (self.kbuf, self.vbuf, self.lfbuf, self.sbuf, self.sem, self.m, self.l, self.acc, self.carry, self.cfin) = scratch
        self.pt, self.oa = pt_ref, oa_ref
        self.ppc, self.nch, self.total, self.b0, self.n_heads, self.dh, self.rh = ppc, nch, total, b0, n_heads, dh, rh
        self.lf_rows = rh // LANES

    def key_copies(self, t, slot):
        out = []
        for j in range(self.ppc):
            pid = self.pt[(self.b0 * self.nch + t) * self.ppc + j]
            out.append(pltpu.make_async_copy(self.ck.at[pid], self.kbuf.at[slot, pl.ds(j * self.rh, self.rh), :],
                                             self.sem.at[0, slot]))
            out.append(pltpu.make_async_copy(self.clf.at[pid],
                                             self.lfbuf.at[slot, pl.ds(j * self.lf_rows, self.lf_rows), :],
                                             self.sem.at[2, slot]))
        return out

    def value_copies(self, t, slot):
        out = []
        for j in range(self.ppc):
            pid = self.pt[(self.b0 * self.nch + t) * self.ppc + j]
            out.append(pltpu.make_async_copy(self.cv.at[pid], self.vbuf.at[slot, pl.ds(j * self.rh, self.rh), :],
                                             self.sem.at[1, slot]))
        return out

    def _bias_and_mask(self, slot):
        n_heads = self.n_heads
        nrow = self.ppc * self.lf_rows
        lfv = self.lfbuf[slot]
        r = jnp.dot(jnp.concatenate(_split3(lfv), axis=0), self.uv[...], preferred_element_type=F32)
        r = r[0:nrow] + r[nrow:2 * nrow] + r[2 * nrow:3 * nrow]
        within, rowtot = r[:, :LANES], r[:, LANES:]
        rr = jnp.dot(self.slt[...], jnp.concatenate(_split3(rowtot), axis=1), preferred_element_type=F32)
        offs = rr[:, :LANES] + rr[:, LANES:2 * LANES] + rr[:, 2 * LANES:]
        carry = self.carry[...]
        cs = within + offs + carry
        self.carry[...] = carry + offs[nrow - 1:nrow, :] + rowtot[nrow - 1:nrow, :]
        return jnp.concatenate([jnp.broadcast_to(cs[i:i + 1, :], (n_heads, LANES)) for i in range(nrow)], axis=1)

    def _head_match(self):
        lane = lax.broadcasted_iota(jnp.int32, (self.n_heads, LANES), 1)
        return lane, jnp.bitwise_and(lane, self.n_heads - 1) == lax.broadcasted_iota(jnp.int32, (self.n_heads, LANES), 0)

    def _scores(self, t, slot):
        q = self.q[self.b0 + t // self.nch]
        s = lax.dot_general(q, self.kbuf[slot], _NT, preferred_element_type=F32)
        self.sbuf[slot] = s - self._bias_and_mask(slot)

    def _update(self, s, valid, v_rows):
        s = jnp.where(valid, s, -jnp.inf)
        m_old = self.m[...]
        m_new = jnp.maximum(m_old, jnp.max(s, axis=-1, keepdims=True))
        alpha = jnp.exp(m_old - m_new)
        pr = jnp.exp(s - m_new)
        self.l[...] = alpha * self.l[...] + jnp.sum(pr, axis=-1, keepdims=True)
        self.acc[...] = alpha * self.acc[...] + jnp.dot(pr, v_rows, preferred_element_type=F32)
        self.m[...] = m_new

    def _reset_softmax(self):
        self.m[...] = jnp.full(self.m.shape, -jnp.inf, F32)
        self.l[...] = jnp.zeros(self.l.shape, F32)
        self.acc[...] = jnp.zeros(self.acc.shape, F32)

    def _finish_batch(self, b):
        n_heads, dh = self.n_heads, self.dh
        lane, head_match = self._head_match()
        bg = self.b0 + b
        q = self.q[bg]
        zpad = jnp.zeros((LANES - n_heads, dh), F32)
        k_self = jnp.concatenate([zpad, self.kn[bg]], axis=0)
        v_self = jnp.concatenate([zpad, self.vn[bg]], axis=0)
        s_self = lax.dot_general(q, k_self, _NT, preferred_element_type=F32)
        c_t = self.cfin[...] + self.lfn[bg]
        self._update(s_self - c_t, jnp.logical_and(lane >= LANES - n_heads, head_match), v_self)
        o = self.acc[...] / self.l[...]
        ms = jnp.sum(jnp.sum(o * o, axis=-1, keepdims=True), axis=0, keepdims=True) / (n_heads * dh)
        self.oa[b] = o * lax.rsqrt(ms + EPS) * self.g[...]

    def start_first(self):
        for t in range(min(2, self.total)):
            for cp in self.key_copies(t, t) + self.value_copies(t, t):
                cp.start()
        for cp in self.key_copies(0, 0):
            cp.wait()
        self.carry[...] = jnp.zeros(self.carry.shape, F32)
        self._scores(0, 0)

        @pl.when(2 < self.total)
        def _():
            for cp in self.key_copies(2, 0):
                cp.start()

    def call(self, u, with_keys):
        nch = self.nch
        tv = u - 1
        bv = tv // nch
        cv = tv - bv * nch
        vslot = lax.rem(tv, 2)
        kslot = 1 - vslot
        if with_keys:
            for cp in self.key_copies(u, kslot):
                cp.wait()
        for cp in self.value_copies(tv, vslot):
            cp.wait()

        first_of_batch = cv == nch - 1 if with_keys else True

        @pl.when(first_of_batch)
        def _():
            self.cfin[...] = self.carry[...]
            self.carry[...] = jnp.zeros(self.carry.shape, F32)

        @pl.when(cv == 0)
        def _():
            self._reset_softmax()

        s_prev = self.sbuf[vslot]
        if with_keys:
            self._scores(u, kslot)
        _, head_match = self._head_match()
        valid = jnp.concatenate([head_match] * (self.ppc * self.lf_rows), axis=1)
        self._update(s_prev, valid, self.vbuf[vslot])

        if with_keys:
            @pl.when(u + 2 < self.total)
            def _():
                for cp in self.key_copies(u + 2, kslot):
                    cp.start()

        @pl.when(u + 1 < self.total)
        def _():
            for cp in self.value_copies(u + 1, vslot):
                cp.start()

        @pl.when(cv == nch - 1)
        def _():
            self._finish_batch(bv)


def _emit_decode_calls(dec, step, cps, slots):
    tail_step, tail_j = (dec.total - 1) // cps, (dec.total - 1) % cps
    for j in slots:
        u = step * cps + j + 1

        @pl.when(u < dec.total)
        def _(u=u):
            dec.call(u, True)

        if j == tail_j:
            @pl.when(step == tail_step)
            def _(u=u):
                dec.call(u, False)


def _decode_consts(nrow, n_heads):
    li = np.arange(LANES)
    same_head = (li[:, None] % n_heads) == (li[None, :] % n_heads)
    u = same_head & (li[:, None] <= li[None, :])
    ri = np.arange(nrow)
    slt = ri[None, :] < ri[:, None]
    as_bf16 = lambda a: jnp.asarray(a.astype(np.float32), BF16)
    return as_bf16(np.concatenate([u, same_head], axis=1)), as_bf16(slt)


N_DECODE_IN = 10


def _decode_plumbing(n_steps, page_table, q, k_new, v_new, lf_new, g_attn, cache_k, cache_v, cache_lf, ppc, b0, b1):
    nb, n_heads, dh = q.shape
    rh = cache_k.shape[1]
    nch = page_table.shape[1] // ppc
    total = (b1 - b0) * nch
    lf_rows = rh // LANES
    uv, slt = _decode_consts(ppc * lf_rows, n_heads)
    dcfg = dict(ppc=ppc, nch=nch, total=total, b0=b0, n_heads=n_heads, dh=dh, rh=rh)
    whole = lambda a: pl.BlockSpec(a.shape, lambda i, j, pt: (0,) * a.ndim)
    args = (q, k_new, v_new, lf_new, g_attn, uv, slt, cache_k, cache_v, cache_lf)
    in_specs = [whole(a) for a in args[:7]] + [pl.BlockSpec(memory_space=pl.ANY)] * 3
    out_spec = pl.BlockSpec((b1 - b0, n_heads, dh), lambda i, j, pt: (0, 0, 0))
    out_shape = jax.ShapeDtypeStruct((b1 - b0, n_heads, dh), F32)
    scratch = [
        pltpu.VMEM((2, ppc * rh, dh), F32), pltpu.VMEM((2, ppc * rh, dh), F32),
        pltpu.VMEM((2, ppc * lf_rows, LANES), F32),
        pltpu.VMEM((2, n_heads, ppc * rh), F32),
        pltpu.SemaphoreType.DMA((3, 2)),
        pltpu.VMEM((n_heads, 1), F32), pltpu.VMEM((n_heads, 1), F32), pltpu.VMEM((n_heads, dh), F32),
        pltpu.VMEM((1, LANES), F32), pltpu.VMEM((1, LANES), F32),
    ]
    return dict(cps=-(-total // n_steps), dcfg=dcfg), args, in_specs, out_spec, out_shape, scratch


def _ffn_decode_body(pt_ref, hf_ref, x1_ref, wg_ref, wu_ref, wd_ref, g_ref, *refs, cps, dcfg):
    dec_in, (o_ref, oa_ref), scratch = refs[:N_DECODE_IN], refs[N_DECODE_IN:N_DECODE_IN + 2], refs[N_DECODE_IN + 2:]
    act_ref, scratch = scratch[0], scratch[1:]
    dec = _Decode(pt_ref, dec_in, oa_ref, scratch, **dcfg)
    f = pl.program_id(1)
    nf = pl.num_programs(1)
    step = pl.program_id(0) * nf + f
    tf = wg_ref.shape[1]
    n_sub = tf // FFN_SUB

    @pl.when(step == 0)
    def _():
        dec.start_first()

    @pl.when(f == 0)
    def _():
        o_ref[...] = x1_ref[...]

    def gate_up(j):
        sl = slice(j * FFN_SUB, (j + 1) * FFN_SUB)
        hf = hf_ref[...]
        gate = jnp.dot(hf, wg_ref[:, sl], preferred_element_type=F32)
        up = jnp.dot(hf, wu_ref[:, sl], preferred_element_type=F32)
        act_ref[:, sl] = (_silu(gate) * up).astype(BF16)

    def down():
        d = o_ref.shape[1]
        for n0 in range(0, d, 512):
            o_ref[:, n0:n0 + 512] += jnp.dot(act_ref[...], wd_ref[:, n0:n0 + 512], preferred_element_type=F32)

    pieces = [functools.partial(gate_up, j) for j in range(n_sub)] + [down]
    per_piece = [cps // len(pieces) + (1 if i < cps % len(pieces) else 0) for i in range(len(pieces))]
    j = 0
    for piece, n_calls in zip(pieces, per_piece):
        _emit_decode_calls(dec, step, cps, range(j, j + n_calls))
        j += n_calls
        piece()

    @pl.when(f == nf - 1)
    def _():
        y = o_ref[...]
        o_ref[...] = y * lax.rsqrt(jnp.mean(y * y, axis=-1, keepdims=True) + EPS) * g_ref[...]


def _ffn_decode(hf, x1, wg, wu, wd, g_final, tm, tf, decode_operands, b0, b1):
    m, d = x1.shape
    dff = wg.shape[1]
    n_steps = (m // tm) * (dff // tf)
    static, dec_args, dec_specs, oa_spec, oa_shape, dec_scratch = _decode_plumbing(n_steps, *decode_operands, b0, b1)
    grid_spec = pltpu.PrefetchScalarGridSpec(
        num_scalar_prefetch=1,
        grid=(m // tm, dff // tf),
        in_specs=[pl.BlockSpec((tm, d), lambda i, f, pt: (i, 0), pipeline_mode=pl.Buffered(1)),
                  pl.BlockSpec((tm, d), lambda i, f, pt: (i, 0), pipeline_mode=pl.Buffered(1)),
                  pl.BlockSpec((d, tf), lambda i, f, pt: (0, f)), pl.BlockSpec((d, tf), lambda i, f, pt: (0, f)),
                  pl.BlockSpec((tf, d), lambda i, f, pt: (f, 0)),
                  pl.BlockSpec(g_final.shape, lambda i, f, pt: (0, 0))] + dec_specs,
        out_specs=[pl.BlockSpec((tm, d), lambda i, f, pt: (i, 0), pipeline_mode=pl.Buffered(1)), oa_spec],
        scratch_shapes=[pltpu.VMEM((tm, tf), BF16)] + dec_scratch,
    )
    return pl.pallas_call(
        functools.partial(_ffn_decode_body, **static),
        grid_spec=grid_spec,
        out_shape=[jax.ShapeDtypeStruct((m, d), F32), oa_shape],
        compiler_params=_params(("arbitrary", "arbitrary")),
        name="ffn_with_paged_decode",
    )(decode_operands[0].reshape(-1), hf, x1, wg, wu, wd, g_final, *dec_args)


def _ssd_step_body(xbc_ref, sc_ref, z_ref, dt_ref, st_ref, cw_ref, cb_ref, arow_ref, e_ref, dsk_ref, g_ref,
                   o_ref, sto_ref, sco_ref, *, ds, n_groups, n_state, n_heads, p, hoff, kw):
    xn = xbc_ref[...]
    a = cw_ref[0:1, :] * sc_ref[0:1, :]
    for i in range(1, kw - 1):
        a = a + cw_ref[i:i + 1, :] * sc_ref[i:i + 1, :]
    a = cb_ref[...] + (a + cw_ref[kw - 1:kw, :] * xn)
    act = _silu(a)
    for i in range(kw - 2):
        sco_ref[i:i + 1, :] = sc_ref[i + 1:i + 2, :]
    sco_ref[kw - 2:kw - 1, :] = xn

    dt = dt_ref[...]
    da = dt * arow_ref[...]
    dt_exp = _dot_r01(jnp.broadcast_to(dt, (SUBLANES, LANES)), e_ref[...])[0:1, :]
    decay = jnp.exp(da)
    xs = act[:, :ds]
    xdt = xs * dt_exp
    hpg = n_heads // n_groups
    gw = hpg * p
    eye = lax.broadcasted_iota(jnp.int32, (gw, gw), 0) == lax.broadcasted_iota(jnp.int32, (gw, gw), 1)
    y_parts = []
    for g in range(n_groups):
        bg = act[:, ds + g * n_state:ds + (g + 1) * n_state]
        cg = act[:, ds + (n_groups + g) * n_state:ds + (n_groups + g + 1) * n_state]
        diag = jnp.where(eye, jnp.broadcast_to(xdt[:, g * gw:(g + 1) * gw], (gw, gw)), 0.0).astype(BF16)
        outer = jnp.dot(diag, jnp.broadcast_to(bg, (gw, n_state)).astype(BF16), preferred_element_type=F32)
        for jj in range(hpg):
            j = g * hpg + jj
            rows = slice(j * p, (j + 1) * p)
            sto_ref[rows, :] = st_ref[rows, :] * decay[:, hoff + j:hoff + j + 1] + outer[jj * p:(jj + 1) * p, :]
        st_new = sto_ref[g * gw:(g + 1) * gw, :].astype(BF16)
        c8 = jnp.broadcast_to(cg, (SUBLANES, n_state)).astype(BF16)
        y_parts.append(lax.dot_general(c8, st_new, _NT, preferred_element_type=F32)[0:1, :])
    y = jnp.concatenate(y_parts, axis=1)
    y = (y + dsk_ref[...] * xs) * _silu(z_ref[...])
    o_ref[...] = (y * lax.rsqrt(jnp.mean(y * y, axis=-1, keepdims=True) + EPS) * g_ref[...]).astype(o_ref.dtype)


def _ssd_step(xbc, state_conv, z, dt, state, conv_w, conv_b, a_row, expand, dskip_row, g_ssd, dims):
    ds, n_groups, n_state, n_heads, p, hoff, kw = dims
    nb, _, dc = xbc.shape
    per_b = lambda b: (b, 0, 0)
    body = functools.partial(_ssd_step_body, ds=ds, n_groups=n_groups, n_state=n_state, n_heads=n_heads,
                             p=p, hoff=hoff, kw=kw)
    return pl.pallas_call(
        body,
        grid=(nb,),
        in_specs=[
            pl.BlockSpec((None, 1, dc), per_b), pl.BlockSpec((None, kw - 1, dc), per_b),
            pl.BlockSpec((None, 1, ds), per_b), pl.BlockSpec((None, 1, LANES), per_b),
            pl.BlockSpec((None, n_heads * p, n_state), per_b),
            _resident((kw, dc)), _resident((1, dc)), _resident((1, LANES)), _resident(expand.shape),
            _resident((1, ds)), _resident((1, ds)),
        ],
        out_specs=[pl.BlockSpec((None, 1, ds), per_b), pl.BlockSpec((None, n_heads * p, n_state), per_b),
                   pl.BlockSpec((None, kw - 1, dc), per_b)],
        out_shape=[jax.ShapeDtypeStruct((nb, 1, ds), BF16), jax.ShapeDtypeStruct(state.shape, F32),
                   jax.ShapeDtypeStruct(state_conv.shape, F32)],
        compiler_params=_params(("arbitrary",)),
        name="ssd_step",
    )(xbc, state_conv, z, dt, state, conv_w, conv_b, a_row, expand, dskip_row, g_ssd)


def _pick_tile(m, pref):
    t = min(m, pref)
    while m % t:
        t //= 2
    return t


def kernel(x_prompt, x_sample, cache_k, cache_v, cache_logf, state_ssm, state_conv, page_table, g_mix, w_in, b_fgate, conv_w, conv_b, dt_bias, a_log, d_skip, g_attn, g_ssd, w_out, g_ffn, w_gate, w_up, w_down, g_final):
    bp, seq, d = x_prompt.shape
    bs = x_sample.shape[0]
    depth, n_pool, page, n_heads, dh = cache_k.shape
    da = n_heads * dh
    nh, p, n_state = state_ssm.shape[2:]
    ds = nh * p
    kw = conv_w.shape[1]
    dc = conv_w.shape[2]
    n_groups = (dc - ds) // (2 * n_state)
    hoff = n_heads
    assert bs >= 2 and depth == 1, "the FFN kernel applies the final rmsnorm, so it must be the last layer"
    assert x_sample.shape[1] == 1 and seq % SSD_CHUNK == 0 and n_heads + nh <= LANES - n_heads
    assert n_heads & (n_heads - 1) == 0
    assert (page * n_heads) % LANES == 0 and p * 2 == LANES and (nh // n_groups) % 2 == 0
    dims = (ds, n_groups, n_state, nh, p, hoff, kw)
    o_f, o_z, o_x, o_dt = 3 * da, 3 * da + n_heads, 3 * da + n_heads + ds, 3 * da + n_heads + ds + dc
    mp = bp * seq

    expand_np = np.zeros((LANES, ds), np.float32)
    for j in range(nh):
        expand_np[hoff + j, j * p:(j + 1) * p] = 1.0
    expand = jnp.asarray(expand_np, BF16)

    yp = x_prompt.reshape(mp, d)
    ys = x_sample.reshape(bs, d)
    outs = [[] for _ in range(10)]
    for l in range(depth):
        w = w_in[l]
        w_q, w_k, w_v = (w[:, i * da:(i + 1) * da].astype(BF16) for i in range(3))
        w_f = w[:, o_f:o_f + n_heads]
        w_fd = jnp.concatenate([w_f, w[:, o_dt:o_dt + nh], jnp.zeros((d, LANES - 2 * n_heads - nh), w.dtype), w_f],
                               axis=1).astype(BF16)
        w_z = w[:, o_z:o_z + ds].astype(BF16)
        w_x = w[:, o_x:o_x + dc].astype(BF16)
        w_oa = w_out[l][:da].astype(BF16)
        w_os = w_out[l][da:].astype(BF16)
        wg, wu, wd = w_gate[l].astype(BF16), w_up[l].astype(BF16), w_down[l].astype(BF16)
        zpad = jnp.zeros((LANES - 2 * n_heads - nh,), F32)
        bias_row = jnp.concatenate([b_fgate[l], dt_bias[l], zpad, b_fgate[l]]).reshape(1, LANES)
        a_row = jnp.concatenate([jnp.zeros((n_heads,), F32), -jnp.exp(a_log[l].astype(F32)),
                                 jnp.zeros((LANES - n_heads - nh,), F32)]).reshape(1, LANES)
        dskip_row = jnp.repeat(d_skip[l].astype(F32), p).reshape(1, ds)
        gm, ga, gs, gf = (g_mix[l].reshape(1, d), g_attn[l].reshape(1, da), g_ssd[l].reshape(1, ds),
                          g_ffn[l].reshape(1, d))
        gfin = g_final.reshape(1, d)
        cw, cb = conv_w[l].astype(F32), conv_b[l].reshape(1, dc).astype(F32)
        proj_a = dict(weights=[w_q, w_k, w_v, w_fd], out_map=[[0], [1, 3], [2, 4], [5]],
                      out_dtypes=[BF16, F32, F32, BF16, BF16, F32])
        proj_b = dict(weights=[w_z, w_x], out_map=[[0], [1]], out_dtypes=[F32, F32])

        tm = _pick_tile(mp, 512)
        q, k, v, kb, vb, fd = _norm_proj(yp, gm, tm=tm, name="in_proj_qkv", out_scale={0: dh ** -0.5 * LOG2E}, **proj_a)
        z, xbc = _norm_proj(yp, gm, tm=_pick_tile(mp, 256), name="in_proj_ssd", **proj_b)
        qs, k_s, v_s, _, _, fds = _norm_proj(ys, gm, tm=bs, name="in_proj_qkv_s", out_scale={0: dh ** -0.5}, **proj_a)
        zs, xbcs = _norm_proj(ys, gm, tm=bs, name="in_proj_ssd_s", **proj_b)
        lf, dt, cs, cst = _gates(fd, bias_row, a_row, bp, seq, n_heads)
        lfs, dts = _sgates(fds, bias_row)
        decode_operands = (page_table, qs.astype(F32).reshape(bs, n_heads, dh), k_s.reshape(bs, n_heads, dh),
                           v_s.reshape(bs, n_heads, dh), lfs.reshape(bs, 1, LANES), ga.reshape(n_heads, dh),
                           cache_k[l].reshape(n_pool, page * n_heads, dh), cache_v[l].reshape(n_pool, page * n_heads, dh),
                           cache_logf[l].astype(F32).reshape(n_pool, page * n_heads // LANES, LANES),
                           _pick_tile(page_table.shape[1], DECODE_PAGES_PER_CHUNK))
        b_split = max(1, bs // DECODE_ATTENTION_SHARE)

        oa, oas_a = _attention(q, kb, vb, cst, ga, bp, seq, n_heads, dh, _pick_tile(seq, 256), decode_operands, 0, b_split)
        osm, st_p = _ssd_prompt(xbc, z, dt, cs, cst, cw, cb, dskip_row, gs, bp, seq, dims)
        x1, hf = _outproj(oa, osm, yp, w_oa, w_os, gf, tm, "out_proj")
        tf = _pick_tile(wg.shape[1], 512)
        yp_new, oas_f = _ffn_decode(hf, x1, wg, wu, wd, gfin, _pick_tile(mp, 1024), tf, decode_operands, b_split, bs)
        oas = jnp.concatenate([oas_a, oas_f], axis=0)

        oss, st_s, conv_s = _ssd_step(xbcs.reshape(bs, 1, dc), state_conv[l].astype(F32), zs.reshape(bs, 1, ds),
                                      dts.reshape(bs, 1, LANES), state_ssm[l].astype(F32).reshape(bs, nh * p, n_state),
                                      cw, cb, a_row, expand, dskip_row, gs, dims)
        x1s, hfs = _outproj(oas.astype(BF16).reshape(bs, da), oss.reshape(bs, ds), ys, w_oa, w_os, gf, bs, "out_proj_s")
        ys_new = _ffn(hfs, x1s, wg, wu, wd, gfin, bs, tf, "ffn_s")

        yp, ys = yp_new, ys_new
        new = [k.reshape(bp, seq, n_heads, dh), v.reshape(bp, seq, n_heads, dh),
               lf[:, :n_heads].reshape(bp, seq, n_heads),
               k_s.reshape(bs, 1, n_heads, dh), v_s.reshape(bs, 1, n_heads, dh),
               lfs[:, :n_heads].reshape(bs, 1, n_heads),
               st_p.reshape(bp, nh, p, n_state), xbc.reshape(bp, seq, dc)[:, seq - kw + 1:],
               st_s.reshape(bs, nh, p, n_state), conv_s]
        for acc, val in zip(outs, new):
            acc.append(val)

    return (yp.reshape(bp, seq, d), ys.reshape(bs, 1, d)) + tuple(jnp.stack(o) for o in outs)
```
